```python
import jax, jax.numpy as jnp
from jax import lax
import numpy as np

D_MODEL = 1024
BATCH = 4
SEQ = 4096
DEPTH = 4
DEC_BATCH = 128
DEC_SEQ = 1
PAST_LEN = 8192
PAGE_SIZE = 128

N_META = 16
V_DIM = 64
NOPE_DIM = 64
ROPE_DIM = 32
QK_DIM = NOPE_DIM + ROPE_DIM
H_A = D_MODEL // (2 * V_DIM)
Q_LORA = 384
KV_LORA = 256
ROPE_BASE = 10000.0
Q_BLOCK = 128
ATTN_SCALE = QK_DIM ** -0.5
N_R = 64
H_R = D_MODEL // (2 * N_R)
RW_DIM = H_R * N_R
DECAY_LORA = 64
AAA_LORA = 64
GATE_LORA = 128
LNX_EPS = 64e-5
ATT_DIM = H_A * V_DIM
MIX_DIM = ATT_DIM + RW_DIM
MLA_COLS = Q_LORA + KV_LORA + ROPE_DIM
RW_COLS = 3 * RW_DIM + DECAY_LORA + AAA_LORA + GATE_LORA
IN_COLS = MLA_COLS + RW_COLS
D_FF = 4 * D_MODEL
NORM_EPS = 1e-6

kernel_name = 'hymba_mla_rwkv7_decoder_step'


def rmsnorm(x, g):
    xf = x.astype(jnp.float32)
    y = xf * lax.rsqrt(jnp.mean(xf * xf, -1, keepdims=True) + NORM_EPS)
    return (y * g.astype(jnp.float32)).astype(x.dtype)


def rope(x, pos):
    half = ROPE_DIM // 2
    inv = ROPE_BASE ** (-jnp.arange(half, dtype=jnp.float32) / half)
    ang = pos.astype(jnp.float32)[:, None] * inv[None, :]
    cos = jnp.cos(ang)[None, :, None, :]
    sin = jnp.sin(ang)[None, :, None, :]
    xf = x.astype(jnp.float32)
    x1, x2 = xf[..., :half], xf[..., half:]
    return jnp.concatenate([x1 * cos - x2 * sin, x1 * sin + x2 * cos], -1).astype(x.dtype)


def project(h, g_pre, w_in, g_cq, g_ckv, pos):
    p = rmsnorm(h, g_pre) @ w_in
    c_q = rmsnorm(p[..., :Q_LORA], g_cq)
    c_kv = rmsnorm(p[..., Q_LORA:Q_LORA + KV_LORA], g_ckv)
    k_rope = rope(p[..., Q_LORA + KV_LORA:MLA_COLS][:, :, None, :], pos)[:, :, 0]
    return c_q, c_kv, k_rope, p[..., MLA_COLS:]


def mla_queries(c_q, w_uq, pos):
    B, T = c_q.shape[:2]
    q = (c_q @ w_uq).reshape(B, T, H_A, QK_DIM)
    return q[..., :NOPE_DIM], rope(q[..., NOPE_DIM:], pos)


def mla_prompt(q_nope, q_rope, c_kv, k_rope, w_ukv):
    B, L = c_kv.shape[:2]
    kv = (c_kv @ w_ukv).reshape(B, L, H_A, NOPE_DIM + V_DIM)
    k = jnp.concatenate([kv[..., :NOPE_DIM],
                         jnp.broadcast_to(k_rope[:, :, None, :], (B, L, H_A, ROPE_DIM))], -1)
    v = kv[..., NOPE_DIM:]
    q = jnp.concatenate([q_nope, q_rope], -1)
    Lp = -(-L // Q_BLOCK) * Q_BLOCK
    padw = ((0, 0), (0, Lp - L), (0, 0), (0, 0))
    q, k, v = jnp.pad(q, padw), jnp.pad(k, padw), jnp.pad(v, padw)
    nb = Lp // Q_BLOCK
    qb = jnp.moveaxis(q.reshape(B, nb, Q_BLOCK, H_A, QK_DIM), 1, 0)
    kpos = jnp.arange(Lp)

    def block(args):
        qi, i = args
        s = jnp.einsum('bqhd,bkhd->bhqk', qi, k).astype(jnp.float32) * ATTN_SCALE
        qpos = i * Q_BLOCK + jnp.arange(Q_BLOCK)
        s = jnp.where(kpos[None, :] <= qpos[:, None], s, -jnp.inf)
        p = jax.nn.softmax(s, -1).astype(v.dtype)
        return jnp.einsum('bhqk,bkhd->bqhd', p, v)

    o = lax.map(block, (qb, jnp.arange(nb)))
    return jnp.moveaxis(o, 0, 1).reshape(B, Lp, ATT_DIM)[:, :L]


def mla_sample(q_nope, q_rope, ckv_new, kr_new, ckv_past, kr_past, w_ukv):
    B, T = q_nope.shape[:2]
    w = w_ukv.reshape(KV_LORA, H_A, NOPE_DIM + V_DIM)
    w_uk, w_uv = w[..., :NOPE_DIM], w[..., NOPE_DIM:]
    q_lat = jnp.einsum('bthd,chd->bthc', q_nope, w_uk)
    s_past = (jnp.einsum('bthc,bsc->bhts', q_lat, ckv_past)
              + jnp.einsum('bthr,bsr->bhts', q_rope, kr_past)).astype(jnp.float32)
    s_new = (jnp.einsum('bthc,bsc->bhts', q_lat, ckv_new)
             + jnp.einsum('bthr,bsr->bhts', q_rope, kr_new)).astype(jnp.float32)
    causal = jnp.arange(T)[None, :] <= jnp.arange(T)[:, None]
    s_new = jnp.where(causal, s_new, -jnp.inf)
    p = jax.nn.softmax(jnp.concatenate([s_past, s_new], -1) * ATTN_SCALE, -1).astype(ckv_new.dtype)
    P = ckv_past.shape[1]
    o_lat = (jnp.einsum('bhts,bsc->bthc', p[..., :P], ckv_past)
             + jnp.einsum('bhts,bsc->bthc', p[..., P:], ckv_new))
    return jnp.einsum('bthc,chd->bthd', o_lat, w_uv).reshape(B, T, ATT_DIM)


def wkv_scan(r, w, k, v, a, b, s0):
    def step(S, inp):
        r_t, w_t, k_t, v_t, a_t, b_t = inp
        sa = jnp.einsum('bhvk,bhk->bhv', S, a_t)
        S = S * w_t[:, :, None, :] + sa[..., None] * b_t[:, :, None, :] + v_t[..., None] * k_t[:, :, None, :]
        return S, jnp.einsum('bhvk,bhk->bhv', S, r_t)
    xs = tuple(jnp.moveaxis(t, 1, 0) for t in (r, w, k, v, a, b))
    S, ys = lax.scan(step, s0, xs, unroll=4)
    return jnp.moveaxis(ys, 0, 1), S


def rwkv7_mix(rw, rw_prev, s0, mu, w0, w2, a0, a2, g2, k_k, k_a, r_k, lnx_w, lnx_b):
    f32 = jnp.float32
    B, T, _ = rw.shape
    x = rw.astype(f32)
    prev = jnp.concatenate([rw_prev.astype(f32)[:, None], x[:, :-1]], 1)
    xs = x + (prev - x) * mu.astype(f32)
    r = xs[..., :RW_DIM]
    k = xs[..., RW_DIM:2 * RW_DIM]
    v = xs[..., 2 * RW_DIM:3 * RW_DIM]
    o = 3 * RW_DIM
    wd = xs[..., o:o + DECAY_LORA]
    ad = xs[..., o + DECAY_LORA:o + DECAY_LORA + AAA_LORA]
    gd = xs[..., o + DECAY_LORA + AAA_LORA:]
    w_log = -jax.nn.softplus(-(w0.astype(f32) + jnp.tanh(wd) @ w2.astype(f32))) - 0.5
    decay = jnp.exp(-jnp.exp(w_log))
    a = jax.nn.sigmoid(a0.astype(f32) + ad @ a2.astype(f32))
    g = jax.nn.sigmoid(gd) @ g2.astype(f32)
    heads = lambda t: t.reshape(B, T, H_R, N_R)
    kk = heads(k * k_k.astype(f32))
    kk = kk * lax.rsqrt(jnp.maximum(jnp.sum(kk * kk, -1, keepdims=True), 1e-24))
    k = k * (1.0 + (a - 1.0) * k_a.astype(f32))
    r_h, k_h, v_h, w_h, a_h = heads(r), heads(k), heads(v), heads(decay), heads(a)
    y, S = wkv_scan(r_h, w_h, k_h, v_h, -kk, kk * a_h, s0.astype(f32))
    mean = jnp.mean(y, -1, keepdims=True)
    var = jnp.mean(jnp.square(y - mean), -1, keepdims=True)
    y = ((y - mean) * lax.rsqrt(var + LNX_EPS)).reshape(B, T, RW_DIM) * lnx_w.astype(f32) + lnx_b.astype(f32)
    bonus = (jnp.sum(r_h * k_h * r_k.astype(f32), -1, keepdims=True) * v_h).reshape(B, T, RW_DIM)
    return ((y + bonus) * g).astype(rw.dtype), S.astype(s0.dtype), rw[:, -1]


def finish(h, o_att, o_rw, g_post, w_out, g_ffn_pre, g_ffn_post, w_up, w_down):
    o = jnp.concatenate([o_att, o_rw], -1) @ w_out
    h = h + rmsnorm(o, g_post)
    u = jnp.square(jax.nn.relu(rmsnorm(h, g_ffn_pre) @ w_up))
    return h + rmsnorm(u @ w_down, g_ffn_post)


def setup_inputs(seed: int = 0) -> dict:
    key = jax.random.key(seed)
    ks = iter(jax.random.split(key, 48))
    f32 = jnp.float32

    def nrm(shape, scale=1.0):
        return jax.random.normal(next(ks), shape, f32) * scale

    def gain(shape):
        return 1.0 + nrm(shape, 0.02)

    n_pages = PAST_LEN // PAGE_SIZE
    n_pool = (DEC_BATCH * n_pages * 5) // 4
    page_table = jax.random.permutation(next(ks), n_pool)[:DEC_BATCH * n_pages]
    page_table = page_table.reshape(DEC_BATCH, n_pages).astype(jnp.int32)
    return {
        'x_prompt': nrm((BATCH, SEQ, D_MODEL)),
        'x_sample': nrm((DEC_BATCH, DEC_SEQ, D_MODEL)),
        'cache_ckv': nrm((DEPTH, n_pool, PAGE_SIZE, KV_LORA)),
        'cache_krope': nrm((DEPTH, n_pool, PAGE_SIZE, ROPE_DIM)),
        'state_wkv': nrm((DEPTH, DEC_BATCH, H_R, N_R, N_R), 0.5),
        'state_shift': nrm((DEPTH, DEC_BATCH, RW_COLS)),
        'page_table': page_table,
        'meta_tokens': nrm((N_META, D_MODEL)),
        'g_mix_pre': gain((DEPTH, D_MODEL)),
        'g_mix_post': gain((DEPTH, D_MODEL)),
        'g_ffn_pre': gain((DEPTH, D_MODEL)),
        'g_ffn_post': gain((DEPTH, D_MODEL)),
        'w_in': nrm((DEPTH, D_MODEL, IN_COLS), D_MODEL ** -0.5),
        'g_cq': gain((DEPTH, Q_LORA)),
        'g_ckv': gain((DEPTH, KV_LORA)),
        'w_uq': nrm((DEPTH, Q_LORA, H_A * QK_DIM), Q_LORA ** -0.5),
        'w_ukv': nrm((DEPTH, KV_LORA, H_A * (NOPE_DIM + V_DIM)), KV_LORA ** -0.5),
        'mu_shift': jax.random.uniform(next(ks), (DEPTH, RW_COLS), f32),
        'w0': -0.5 + nrm((DEPTH, RW_DIM), 0.5),
        'w2': nrm((DEPTH, DECAY_LORA, RW_DIM), 0.5 * DECAY_LORA ** -0.5),
        'a0': nrm((DEPTH, RW_DIM), 0.1),
        'a2': nrm((DEPTH, AAA_LORA, RW_DIM), 0.5 * AAA_LORA ** -0.5),
        'g2': nrm((DEPTH, GATE_LORA, RW_DIM), GATE_LORA ** -0.5),
        'k_k': 0.85 + nrm((DEPTH, RW_DIM), 0.02),
        'k_a': 1.0 + nrm((DEPTH, RW_DIM), 0.02),
        'r_k': nrm((DEPTH, H_R, N_R), 0.1),
        'lnx_w': gain((DEPTH, RW_DIM)),
        'lnx_b': nrm((DEPTH, RW_DIM), 0.02),
        'w_out': nrm((DEPTH, MIX_DIM, D_MODEL), MIX_DIM ** -0.5),
        'w_up': nrm((DEPTH, D_MODEL, D_FF), D_MODEL ** -0.5),
        'w_down': nrm((DEPTH, D_FF, D_MODEL), D_FF ** -0.5),
    }


def reference(x_prompt, x_sample, cache_ckv, cache_krope, state_wkv, state_shift, page_table,
              meta_tokens, g_mix_pre, g_mix_post, g_ffn_pre, g_ffn_post, w_in, g_cq, g_ckv,
              w_uq, w_ukv, mu_shift, w0, w2, a0, a2, g2, k_k, k_a, r_k, lnx_w, lnx_b,
              w_out, w_up, w_down):
    Bp = x_prompt.shape[0]
    Bd, T = x_sample.shape[:2]
    h_p = jnp.concatenate([jnp.broadcast_to(meta_tokens[None].astype(x_prompt.dtype),
                                            (Bp, N_META, D_MODEL)), x_prompt], 1)
    pos_p = jnp.arange(h_p.shape[1])
    pos_s = PAST_LEN + jnp.arange(T)
    h_s = x_sample
    zero_shift = jnp.zeros((Bp, RW_COLS), x_prompt.dtype)
    zero_state = jnp.zeros((Bp, H_R, N_R, N_R), state_wkv.dtype)
    p_ckv, p_kr, p_wkv, p_sh, s_ckv, s_kr, s_wkv, s_sh = [], [], [], [], [], [], [], []
    for l in range(DEPTH):
        rw_args = (mu_shift[l], w0[l], w2[l], a0[l], a2[l], g2[l], k_k[l], k_a[l], r_k[l],
                   lnx_w[l], lnx_b[l])
        out_args = (g_mix_post[l], w_out[l], g_ffn_pre[l], g_ffn_post[l], w_up[l], w_down[l])
        c_q, c_kv, k_r, rw = project(h_p, g_mix_pre[l], w_in[l], g_cq[l], g_ckv[l], pos_p)
        q_nope, q_rope = mla_queries(c_q, w_uq[l], pos_p)
        o_att = mla_prompt(q_nope, q_rope, c_kv, k_r, w_ukv[l])
        o_rw, s_fin, sh_fin = rwkv7_mix(rw, zero_shift, zero_state, *rw_args)
        h_p = finish(h_p, o_att, o_rw, *out_args)
        p_ckv.append(c_kv)
        p_kr.append(k_r)
        p_wkv.append(s_fin)
        p_sh.append(sh_fin)
        c_q, c_kv, k_r, rw = project(h_s, g_mix_pre[l], w_in[l], g_cq[l], g_ckv[l], pos_s)
        q_nope, q_rope = mla_queries(c_q, w_uq[l], pos_s)
        ckv_past = cache_ckv[l, page_table].reshape(Bd, -1, KV_LORA)
        kr_past = cache_krope[l, page_table].reshape(Bd, -1, ROPE_DIM)
        o_att = mla_sample(q_nope, q_rope, c_kv, k_r, ckv_past, kr_past, w_ukv[l])
        o_rw, s_new, sh_new = rwkv7_mix(rw, state_shift[l], state_wkv[l], *rw_args)
        h_s = finish(h_s, o_att, o_rw, *out_args)
        s_ckv.append(c_kv)
        s_kr.append(k_r)
        s_wkv.append(s_new)
        s_sh.append(sh_new)
    y_prompt = h_p[:, N_META:]
    y_sample = h_s
    return (y_prompt, y_sample, jnp.stack(p_ckv), jnp.stack(p_kr), jnp.stack(p_wkv), jnp.stack(p_sh),
            jnp.stack(s_ckv), jnp.stack(s_kr), jnp.stack(s_wkv), jnp.stack(s_sh))
```

```python
import functools
import math

import jax
import jax.numpy as jnp
from jax import lax
from jax.experimental import pallas as pl
from jax.experimental.pallas import tpu as pltpu

F32 = jnp.float32
BF16 = jnp.bfloat16

D_MODEL = 1024
N_META = 16
V_DIM = 64
NOPE_DIM = 64
ROPE_DIM = 32
QK_DIM = NOPE_DIM + ROPE_DIM
H_A = 8
Q_LORA = 384
KV_LORA = 256
ROPE_BASE = 10000.0
ATTN_SCALE = QK_DIM ** -0.5
N_R = 64
H_R = 8
RW_DIM = H_R * N_R
DECAY_LORA = 64
AAA_LORA = 64
GATE_LORA = 128
LNX_EPS = 64e-5
ATT_DIM = H_A * V_DIM
MLA_COLS = Q_LORA + KV_LORA + ROPE_DIM
RW_COLS = 3 * RW_DIM + DECAY_LORA + AAA_LORA + GATE_LORA
D_FF = 4 * D_MODEL
NORM_EPS = 1e-6
PAGE_SIZE = 128

LANES = 128
VMEM_LIMIT = 48 * 1024 * 1024
HEAD_PAD = LANES
QK_PAD = H_A * HEAD_PAD
IN_PAD = Q_LORA + KV_LORA + LANES + RW_COLS
ATT_BLK = 384
RW_CHUNK = 64
RW_STEP = 384
PROMPT_TILE = 512
PAGES_PER_STEP = 16
DEC_TILE = 8

_NT = (((1,), (1,)), ((), ()))
_TN = (((0,), (0,)), ((), ()))
_NN = (((1,), (0,)), ((), ()))


def _dg(a, b, dims=_NN):
    return lax.dot_general(a, b, dims, preferred_element_type=F32)


def _dot(a, b, dims=_NN):
    return _dg(a.astype(BF16), b.astype(BF16), dims)


def _split(x):
    hi = x.astype(BF16)
    lo = (x - hi.astype(F32)).astype(BF16)
    return hi, lo


def _dot3(a, b, dims=_NN):
    ah, al = _split(a)
    bh, bl = _split(b)
    return _dg(ah, bh, dims) + (_dg(ah, bl, dims) + _dg(al, bh, dims))


def _dot2x(a, b_exact):
    ah, al = _split(a)
    return _dg(ah, b_exact) + _dg(al, b_exact)


def _rms(x, g):
    ms = jnp.mean(x * x, axis=-1, keepdims=True)
    return x * lax.rsqrt(ms + NORM_EPS) * g


def _const_spec(shape):
    nd = len(shape)
    return pl.BlockSpec(shape, lambda *_: (0,) * nd, pipeline_mode=pl.Buffered(1))


def _params(sem):
    return pltpu.CompilerParams(dimension_semantics=sem, vmem_limit_bytes=VMEM_LIMIT)


def _rope_chunk(x, tc, tm, tp):
    return x * tc + pltpu.roll(x, LANES - 16, 1) * tm + pltpu.roll(x, 16, 1) * tp


def _proj_kernel(h_ref, gpre_ref, win_ref, gcq_ref, gckv_ref, wuq_ref, wk_ref, wv_ref,
                 tc_ref, tm_ref, tp_ref, q_ref, k_ref, v_ref, ckv_ref, kr_ref, rw_ref):
    xn = _rms(h_ref[...], gpre_ref[...]).astype(BF16)
    tc, tm, tp = tc_ref[...], tm_ref[...], tp_ref[...]
    lane = lax.broadcasted_iota(jnp.int32, tc.shape, 1)

    cq = _rms(_dg(xn, win_ref[:, 0:Q_LORA]), gcq_ref[...])
    qf = _dot(cq, wuq_ref[...])
    for h in range(H_A):
        x = qf[:, h * HEAD_PAD:(h + 1) * HEAD_PAD]
        y = jnp.where(lane < NOPE_DIM, x, _rope_chunk(x, tc, tm, tp)) * ATTN_SCALE
        q_ref[:, h * HEAD_PAD:(h + 1) * HEAD_PAD] = y.astype(BF16)

    ckv = _rms(_dg(xn, win_ref[:, Q_LORA:Q_LORA + KV_LORA]), gckv_ref[...])
    ckv_ref[...] = ckv
    cb = ckv.astype(BF16)
    v_ref[...] = _dg(cb, wv_ref[...]).astype(BF16)
    kf = _dg(cb, wk_ref[...])
    kr0 = Q_LORA + KV_LORA
    ykr = _rope_chunk(_dg(xn, win_ref[:, kr0:kr0 + LANES]), tc, tm, tp)
    kr_ref[...] = ykr[:, 0:ROPE_DIM]
    kadd = jnp.where(lane >= NOPE_DIM, ykr, 0.0)
    for h in range(H_A):
        k_ref[:, h * HEAD_PAD:(h + 1) * HEAD_PAD] = (kf[:, h * HEAD_PAD:(h + 1) * HEAD_PAD] + kadd).astype(BF16)

    rw_ref[...] = _dg(xn, win_ref[:, kr0 + LANES:IN_PAD])


def _proj_call(h, lw, tables, tile):
    rows = h.shape[0]
    row = lambda n: pl.BlockSpec((tile, n), lambda i: (i, 0))
    return pl.pallas_call(
        _proj_kernel,
        grid=(rows // tile,),
        in_specs=[row(D_MODEL), _const_spec((1, D_MODEL)), _const_spec((D_MODEL, IN_PAD)),
                  _const_spec((1, Q_LORA)), _const_spec((1, KV_LORA)), _const_spec((Q_LORA, QK_PAD)),
                  _const_spec((KV_LORA, QK_PAD)), _const_spec((KV_LORA, ATT_DIM)),
                  row(LANES), row(LANES), row(LANES)],
        out_specs=[row(QK_PAD), row(QK_PAD), row(ATT_DIM), row(KV_LORA), row(ROPE_DIM), row(RW_COLS)],
        out_shape=[jax.ShapeDtypeStruct((rows, QK_PAD), BF16), jax.ShapeDtypeStruct((rows, QK_PAD), BF16),
                   jax.ShapeDtypeStruct((rows, ATT_DIM), BF16), jax.ShapeDtypeStruct((rows, KV_LORA), F32),
                   jax.ShapeDtypeStruct((rows, ROPE_DIM), F32), jax.ShapeDtypeStruct((rows, RW_COLS), F32)],
        compiler_params=_params(("parallel",)),
    )(h, lw["g_pre"], lw["w_in"], lw["g_cq"], lw["g_ckv"], lw["w_uq"], lw["w_k"], lw["w_v"], *tables)


def _attn_kernel(q_ref, k_ref, v_ref, o_ref):
    qi = pl.program_id(2)
    q = q_ref[...]
    row = lax.broadcasted_iota(jnp.int32, (ATT_BLK, ATT_BLK), 0)
    col = lax.broadcasted_iota(jnp.int32, (ATT_BLK, ATT_BLK), 1)

    def step(j, carry, masked):
        start = pl.multiple_of(j * ATT_BLK, ATT_BLK)
        ks = k_ref[pl.ds(start, ATT_BLK), :]
        vs = v_ref[pl.ds(start, ATT_BLK), :]
        out = []
        for hh in range(2):
            m, l, acc = carry[hh]
            s = _dg(q[:, hh * HEAD_PAD:(hh + 1) * HEAD_PAD], ks[:, hh * HEAD_PAD:(hh + 1) * HEAD_PAD], _NT)
            if masked:
                s = jnp.where(col <= row, s, -jnp.inf)
            m_new = jnp.maximum(m, jnp.max(s, axis=-1, keepdims=True))
            alpha = jnp.exp(m - m_new)
            p = jnp.exp(s - m_new)
            l = alpha * l + jnp.sum(p, axis=-1, keepdims=True)
            acc = alpha * acc + _dg(p.astype(BF16), vs)
            out.append((m_new, l, acc))
        return tuple(out)

    one = (jnp.full((ATT_BLK, 1), -jnp.inf, F32), jnp.zeros((ATT_BLK, 1), F32),
           jnp.zeros((ATT_BLK, 2 * V_DIM), F32))
    carry = lax.fori_loop(0, qi, lambda j, c: step(j, c, False), (one, one))
    (_, l0, a0), (_, l1, a1) = step(qi, carry, True)
    lane = lax.broadcasted_iota(jnp.int32, (ATT_BLK, 2 * V_DIM), 1)
    o_ref[...] = jnp.where(lane < V_DIM, a0 / l0, a1 / l1).astype(BF16)


def _attn_call(q, k, v, batch, seq_pad):
    nq = seq_pad // ATT_BLK
    return pl.pallas_call(
        _attn_kernel,
        grid=(batch, H_A // 2, nq),
        in_specs=[pl.BlockSpec((ATT_BLK, 2 * HEAD_PAD), lambda b, hp, qi: (b * nq + qi, hp)),
                  pl.BlockSpec((seq_pad, 2 * HEAD_PAD), lambda b, hp, qi: (b, hp)),
                  pl.BlockSpec((seq_pad, 2 * V_DIM), lambda b, hp, qi: (b, hp))],
        out_specs=pl.BlockSpec((ATT_BLK, 2 * V_DIM), lambda b, hp, qi: (b * nq + qi, hp)),
        out_shape=jax.ShapeDtypeStruct((batch * seq_pad, ATT_DIM), BF16),
        compiler_params=_params(("parallel", "parallel", "arbitrary")),
    )(q, k, v)


def _rwkv_pre(x, prev, mu, w0, w2p, a0, a2p, g2, k_k, k_a, g8):
    xs = x + (prev - x) * mu
    r = xs[:, 0:RW_DIM]
    k = xs[:, RW_DIM:2 * RW_DIM]
    v = xs[:, 2 * RW_DIM:3 * RW_DIM]
    wa = xs[:, 3 * RW_DIM:3 * RW_DIM + LANES]
    gd = xs[:, 3 * RW_DIM + LANES:RW_COLS]
    z = w0 + _dot3(jnp.tanh(wa), w2p)
    lw = -math.exp(-0.5) * jax.nn.sigmoid(z)
    a = jax.nn.sigmoid(a0 + _dot3(wa, a2p))
    g = _dot3(jax.nn.sigmoid(gd), g2)
    kkr = k * k_k
    kk = kkr * lax.rsqrt(jnp.maximum(_dot2x(kkr * kkr, g8), 1e-24))
    k2 = k * (1.0 + (a - 1.0) * k_a)
    return r, k2, v, kk, a, lw, g


def _rwkv_post(y, r, k2, v, g, r_k, lnx_w, lnx_b, g8):
    mean = _dot2x(y, g8) * (1.0 / N_R)
    d = y - mean
    var = _dot2x(d * d, g8) * (1.0 / N_R)
    yn = d * lax.rsqrt(var + LNX_EPS) * lnx_w + lnx_b
    bonus = _dot2x(r * k2 * r_k, g8) * v
    return (yn + bonus) * g


def _rwkv_prompt_kernel(seq_real, rw_ref, mu_ref, w0_ref, w2_ref, a0_ref, a2_ref, g2_ref, kk_ref, ka_ref,
                        rk_ref, lnw_ref, lnb_ref, g8_ref, o_ref, st_ref,
                        st_scr, last_scr, rt_scr, at_scr, bt_scr, kt_scr, bw_scr, kw_scr, v_scr, wc_scr, y_scr):
    t = pl.program_id(1)
    C = RW_CHUNK

    @pl.when(t == 0)
    def _():
        st_scr[...] = jnp.zeros_like(st_scr)
        last_scr[...] = jnp.zeros_like(last_scr)

    x = rw_ref[...]
    rows = lax.broadcasted_iota(jnp.int32, (RW_STEP, 1), 0)
    prev = jnp.where(rows == 0, last_scr[...], pltpu.roll(x, 1, 0))
    last_scr[...] = x[RW_STEP - 1:RW_STEP, :]
    g8 = g8_ref[...]
    r, k2, v, kk, a, lw, g = _rwkv_pre(x, prev, mu_ref[...], w0_ref[...], w2_ref[...], a0_ref[...],
                                       a2_ref[...], g2_ref[...], kk_ref[...], ka_ref[...], g8)
    valid = (t * RW_STEP + rows) < seq_real
    lw = jnp.where(valid, lw, 0.0)
    kk = jnp.where(valid, kk, 0.0)
    k2m = jnp.where(valid, k2, 0.0)

    ri = lax.broadcasted_iota(jnp.int32, (RW_STEP, RW_STEP), 0)
    ci = lax.broadcasted_iota(jnp.int32, (RW_STEP, RW_STEP), 1)
    same = (ri // C) == (ci // C)
    tri = jnp.where(same & (ci <= ri), 1.0, 0.0).astype(BF16)
    blk = jnp.where(same, 1.0, 0.0).astype(BF16)
    l1 = lw.astype(BF16)
    r1 = lw - l1.astype(F32)
    l2 = r1.astype(BF16)
    l3 = (r1 - l2.astype(F32)).astype(BF16)
    cum = _dg(tri, l1) + (_dg(tri, l2) + _dg(tri, l3))
    tot = _dg(blk, l1) + (_dg(blk, l2) + _dg(blk, l3))
    einv = jnp.exp(-cum)
    etail = jnp.exp(tot - cum)
    kb = kk * a
    rt_scr[...] = r * jnp.exp(cum)
    at_scr[...] = -kk * jnp.exp(cum - lw)
    bt_scr[...] = kb * einv
    kt_scr[...] = k2m * einv
    bw_scr[...] = kb * etail
    kw_scr[...] = k2m * etail
    v_scr[...] = v
    wc_scr[...] = jnp.exp(tot)

    r64 = lax.broadcasted_iota(jnp.int32, (C, C), 0)
    c64 = lax.broadcasted_iota(jnp.int32, (C, C), 1)
    strict = c64 < r64
    eye = c64 == r64
    r128 = lax.broadcasted_iota(jnp.int32, (C, 2 * C), 0)
    c128 = lax.broadcasted_iota(jnp.int32, (C, 2 * C), 1)
    incl2 = jnp.where(c128 >= C, c128 - C, c128) <= r128
    zeros64 = jnp.zeros((C, C), F32)

    def chunk(c, carry):
        rs = pl.ds(pl.multiple_of(c * C, C), C)
        for h in range(H_R):
            ls = slice(h * N_R, (h + 1) * N_R)
            at_h, rt_h, bt_h, kt_h = at_scr[rs, ls], rt_scr[rs, ls], bt_scr[rs, ls], kt_scr[rs, ls]
            v_h, bw_h, kw_h = v_scr[rs, ls], bw_scr[rs, ls], kw_scr[rs, ls]
            wc_h = wc_scr[rs, ls]
            sc = _dot3(jnp.concatenate([at_h, rt_h], 0), jnp.concatenate([bt_h, kt_h], 0), _NT)
            n_ab = jnp.where(strict, sc[0:C, 0:C], 0.0)
            a_ak = jnp.where(strict, sc[0:C, C:2 * C], 0.0)
            a_rbk = jnp.where(incl2, sc[C:2 * C, :], 0.0)
            xx = jnp.concatenate([at_h, _dot3(a_ak, v_h)], 1)
            npow = n_ab
            for j in range(6):
                xx = xx + _dot3(npow, xx)
                if j < 5:
                    npow = _dot3(npow, npow)
            zz = jnp.concatenate([xx, jnp.concatenate([zeros64, v_h], 1)], 0)
            yz = _dot3(a_rbk, zz)
            mz = _dot3(jnp.concatenate([bw_h, kw_h], 0), zz, _TN)
            st = st_scr[h]
            y_scr[rs, ls] = _dot3(rt_h + yz[:, 0:C], st) + yz[:, C:2 * C]
            m = jnp.where(eye, wc_h, 0.0) + mz[:, 0:C]
            st_scr[h] = _dot3(m, st) + mz[:, C:2 * C]
        return carry

    lax.fori_loop(0, RW_STEP // C, chunk, 0)

    out = _rwkv_post(y_scr[...], r, k2, v, g, rk_ref[...], lnw_ref[...], lnb_ref[...], g8)
    o_ref[...] = out.astype(BF16)

    @pl.when(t == pl.num_programs(1) - 1)
    def _():
        st_ref[0] = st_scr[...]


def _rwkv_prompt_call(rw, lw, batch, seq_pad, seq_real):
    nt = seq_pad // RW_STEP
    vec = lambda n: _const_spec((1, n))
    big = pltpu.VMEM((RW_STEP, RW_DIM), F32)
    return pl.pallas_call(
        functools.partial(_rwkv_prompt_kernel, seq_real),
        grid=(batch, nt),
        in_specs=[pl.BlockSpec((RW_STEP, RW_COLS), lambda b, t: (b * nt + t, 0)),
                  vec(RW_COLS), vec(RW_DIM), _const_spec((LANES, RW_DIM)), vec(RW_DIM),
                  _const_spec((LANES, RW_DIM)), _const_spec((GATE_LORA, RW_DIM)), vec(RW_DIM), vec(RW_DIM),
                  vec(RW_DIM), vec(RW_DIM), vec(RW_DIM), _const_spec((RW_DIM, RW_DIM))],
        out_specs=[pl.BlockSpec((RW_STEP, RW_DIM), lambda b, t: (b * nt + t, 0)),
                   pl.BlockSpec((1, H_R, N_R, N_R), lambda b, t: (b, 0, 0, 0))],
        out_shape=[jax.ShapeDtypeStruct((batch * seq_pad, RW_DIM), BF16),
                   jax.ShapeDtypeStruct((batch, H_R, N_R, N_R), F32)],
        scratch_shapes=[pltpu.VMEM((H_R, N_R, N_R), F32), pltpu.VMEM((1, RW_COLS), F32),
                        big, big, big, big, big, big, big, big, big],
        compiler_params=_params(("arbitrary", "arbitrary")),
    )(rw, lw["mu"], lw["w0"], lw["w2"], lw["a0"], lw["a2"], lw["g2"], lw["k_k"], lw["k_a"],
      lw["r_k"], lw["lnx_w"], lw["lnx_b"], lw["g8"])


def _rwkv_step_kernel(rw_ref, sh_ref, s_ref, mu_ref, w0_ref, w2_ref, a0_ref, a2_ref, g2_ref, kk_ref, ka_ref,
                      rk_ref, lnw_ref, lnb_ref, g8_ref, o_ref, so_ref):
    g8 = g8_ref[...]
    r, k2, v, kk, a, lw, g = _rwkv_pre(rw_ref[...], sh_ref[...], mu_ref[...], w0_ref[...], w2_ref[...],
                                       a0_ref[...], a2_ref[...], g2_ref[...], kk_ref[...], ka_ref[...], g8)
    w = jnp.exp(lw)
    av = -kk
    bv = kk * a
    lane = lax.broadcasted_iota(jnp.int32, (N_R, LANES), 1)
    zpad = jnp.zeros((LANES - DEC_TILE, LANES), F32)
    ys = []
    for p in range(H_R // 2):
        vt = jnp.concatenate([v[:, p * LANES:(p + 1) * LANES], zpad], 0).T
        ycols = []
        for hh in range(2):
            h = 2 * p + hh
            ls = slice(h * N_R, (h + 1) * N_R)
            ymat = jnp.zeros((N_R, LANES), F32)
            for b in range(DEC_TILE):
                s = s_ref[b, h]
                sa = jnp.sum(s * av[b:b + 1, ls], axis=1, keepdims=True)
                vcol = vt[hh * N_R:(hh + 1) * N_R, b:b + 1]
                s2 = s * w[b:b + 1, ls] + sa * bv[b:b + 1, ls] + vcol * k2[b:b + 1, ls]
                so_ref[b, h] = s2
                ycol = jnp.sum(s2 * r[b:b + 1, ls], axis=1, keepdims=True)
                ymat = jnp.where(lane == b, ycol, ymat)
            ycols.append(ymat)
        ys.append(jnp.concatenate(ycols, 0).T[0:DEC_TILE, :])
    y = jnp.concatenate(ys, 1)
    o_ref[...] = _rwkv_post(y, r, k2, v, g, rk_ref[...], lnw_ref[...], lnb_ref[...], g8)


def _rwkv_step_call(rw, shift, state_all, layer, lw):
    nb = rw.shape[0]
    vec = lambda n: _const_spec((1, n))
    return pl.pallas_call(
        _rwkv_step_kernel,
        grid=(nb // DEC_TILE,),
        in_specs=[pl.BlockSpec((DEC_TILE, RW_COLS), lambda i: (i, 0)),
                  pl.BlockSpec((DEC_TILE, RW_COLS), lambda i: (i, 0)),
                  pl.BlockSpec((None, DEC_TILE, H_R, N_R, N_R), lambda i: (layer, i, 0, 0, 0)),
                  vec(RW_COLS), vec(RW_DIM), _const_spec((LANES, RW_DIM)), vec(RW_DIM),
                  _const_spec((LANES, RW_DIM)), _const_spec((GATE_LORA, RW_DIM)), vec(RW_DIM), vec(RW_DIM),
                  vec(RW_DIM), vec(RW_DIM), vec(RW_DIM), _const_spec((RW_DIM, RW_DIM))],
        out_specs=[pl.BlockSpec((DEC_TILE, RW_DIM), lambda i: (i, 0)),
                   pl.BlockSpec((DEC_TILE, H_R, N_R, N_R), lambda i: (i, 0, 0, 0))],
        out_shape=[jax.ShapeDtypeStruct((nb, RW_DIM), F32),
                   jax.ShapeDtypeStruct((nb, H_R, N_R, N_R), F32)],
        compiler_params=_params(("parallel",)),
    )(rw, shift, state_all, lw["mu"], lw["w0"], lw["w2"], lw["a0"], lw["a2"], lw["g2"], lw["k_k"],
      lw["k_a"], lw["r_k"], lw["lnx_w"], lw["lnx_b"], lw["g8"])


def _finish_kernel(h_ref, oa_ref, orw_ref, woa_ref, wor_ref, gpost_ref, gfpre_ref, gfpost_ref,
                   wup_ref, wdn_ref, out_ref):
    o = _dot(oa_ref[...], woa_ref[...]) + _dot(orw_ref[...], wor_ref[...])
    h1 = h_ref[...] + _rms(o, gpost_ref[...])
    xn = _rms(h1, gfpre_ref[...]).astype(BF16)
    acc = jnp.zeros(h1.shape, F32)
    for c in range(D_FF // D_MODEL):
        cs = slice(c * D_MODEL, (c + 1) * D_MODEL)
        u = jnp.square(jnp.maximum(_dg(xn, wup_ref[:, cs]), 0.0))
        acc = acc + _dg(u.astype(BF16), wdn_ref[cs, :])
    out_ref[...] = h1 + _rms(acc, gfpost_ref[...])


def _finish_call(h, o_att, o_rw, lw, tile):
    rows = h.shape[0]
    row = lambda n: pl.BlockSpec((tile, n), lambda i: (i, 0))
    vec = _const_spec((1, D_MODEL))
    return pl.pallas_call(
        _finish_kernel,
        grid=(rows // tile,),
        in_specs=[row(D_MODEL), row(ATT_DIM), row(RW_DIM), _const_spec((ATT_DIM, D_MODEL)),
                  _const_spec((RW_DIM, D_MODEL)), vec, vec, vec,
                  _const_spec((D_MODEL, D_FF)), _const_spec((D_FF, D_MODEL))],
        out_specs=row(D_MODEL),
        out_shape=jax.ShapeDtypeStruct((rows, D_MODEL), F32),
        compiler_params=_params(("parallel",)),
    )(h, o_att, o_rw, lw["w_out_att"], lw["w_out_rw"], lw["g_post"], lw["g_ffn_pre"], lw["g_ffn_post"],
      lw["w_up"], lw["w_down"])


def _qlat_kernel(q_ref, wuk_ref, o_ref):
    q = q_ref[...]
    for h in range(H_A):
        o_ref[:, h * KV_LORA:(h + 1) * KV_LORA] = _dg(q[:, h * HEAD_PAD:(h + 1) * HEAD_PAD], wuk_ref[h])


def _qlat_call(q, w_uk):
    nb = q.shape[0]
    return pl.pallas_call(
        _qlat_kernel,
        out_shape=jax.ShapeDtypeStruct((nb, H_A * KV_LORA), F32),
    )(q, w_uk)


def _ouv_kernel(ol_ref, wuv_ref, o_ref):
    ol = ol_ref[...]
    acc = jnp.zeros(o_ref.shape, F32)
    for h in range(H_A):
        acc = acc + _dot(ol[:, h * KV_LORA:(h + 1) * KV_LORA], wuv_ref[h])
    o_ref[...] = acc


def _ouv_call(o_lat, w_uv):
    nb = o_lat.shape[0]
    return pl.pallas_call(
        _ouv_kernel,
        out_shape=jax.ShapeDtypeStruct((nb, ATT_DIM), F32),
    )(o_lat, w_uv)


def _decode_attn_kernel(pt_ref, ql_ref, qr_ref, cn_ref, kn_ref, *rest):
    n = PAGES_PER_STEP
    ckv_refs, kr_refs = rest[0:n], rest[n:2 * n]
    o_ref, m_scr, l_scr, acc_scr = rest[2 * n:]
    j = pl.program_id(1)

    @pl.when(j == 0)
    def _():
        m_scr[...] = jnp.full_like(m_scr, -jnp.inf)
        l_scr[...] = jnp.zeros_like(l_scr)
        acc_scr[...] = jnp.zeros_like(acc_scr)

    ql = ql_ref[0]
    qr = qr_ref[0]
    qlb, qrb = ql.astype(BF16), qr.astype(BF16)
    scores, pages = [], []
    for i in range(n):
        cb = ckv_refs[i][...].astype(BF16)
        kb = kr_refs[i][...].astype(BF16)
        scores.append(_dg(qlb, cb, _NT) + _dg(qrb, kb, _NT))
        pages.append(cb)
    s = jnp.concatenate(scores, 1)
    m_old = m_scr[...]
    m_new = jnp.maximum(m_old, jnp.max(s, axis=-1, keepdims=True))
    alpha = jnp.exp(m_old - m_new)
    p = jnp.exp(s - m_new)
    pb = p.astype(BF16)
    pv = _dg(pb[:, 0:PAGE_SIZE], pages[0])
    for i in range(1, n):
        pv = pv + _dg(pb[:, i * PAGE_SIZE:(i + 1) * PAGE_SIZE], pages[i])
    l_new = alpha * l_scr[...] + jnp.sum(p, axis=-1, keepdims=True)
    acc_new = alpha * acc_scr[...] + pv
    m_scr[...] = m_new
    l_scr[...] = l_new
    acc_scr[...] = acc_new

    @pl.when(j == pl.num_programs(1) - 1)
    def _():
        cn = cn_ref[0]
        kn = kn_ref[0]
        s_n = jnp.sum(ql * cn, axis=-1, keepdims=True) + jnp.sum(qr * kn, axis=-1, keepdims=True)
        m2 = jnp.maximum(m_new, s_n)
        a2 = jnp.exp(m_new - m2)
        pn = jnp.exp(s_n - m2)
        o_ref[0] = (a2 * acc_new + pn * cn) / (a2 * l_new + pn)


def _decode_attn_call(q_lat, q_rope, ckv_new, kr_new, cache_ckv, cache_krope, page_table, layer):
    nb, n_pages = page_table.shape
    n = PAGES_PER_STEP

    def page_spec(width, i):
        return pl.BlockSpec((None, None, PAGE_SIZE, width),
                            lambda b, j, pt: (layer, pt[b, j * n + i], 0, 0))

    per_b = lambda shape: pl.BlockSpec((1,) + shape, lambda b, j, pt: (b, 0, 0))
    grid_spec = pltpu.PrefetchScalarGridSpec(
        num_scalar_prefetch=1,
        grid=(nb, n_pages // n),
        in_specs=[per_b((H_A, KV_LORA)), per_b((H_A, ROPE_DIM)), per_b((1, KV_LORA)), per_b((1, ROPE_DIM))]
                 + [page_spec(KV_LORA, i) for i in range(n)] + [page_spec(ROPE_DIM, i) for i in range(n)],
        out_specs=per_b((H_A, KV_LORA)),
        scratch_shapes=[pltpu.VMEM((H_A, 1), F32), pltpu.VMEM((H_A, 1), F32), pltpu.VMEM((H_A, KV_LORA), F32)],
    )
    return pl.pallas_call(
        _decode_attn_kernel,
        grid_spec=grid_spec,
        out_shape=jax.ShapeDtypeStruct((nb, H_A, KV_LORA), F32),
        compiler_params=_params(("parallel", "arbitrary")),
    )(page_table, q_lat, q_rope, ckv_new, kr_new, *([cache_ckv] * n), *([cache_krope] * n))


def _rope_tables(pos):
    half = ROPE_DIM // 2
    inv = ROPE_BASE ** (-jnp.arange(half, dtype=F32) / half)
    ang = pos.astype(F32)[:, None] * inv[None, :]
    cos, sin = jnp.cos(ang), jnp.sin(ang)
    z16, z32 = jnp.zeros_like(cos), jnp.zeros((pos.shape[0], ROPE_DIM), F32)
    seg_c = jnp.concatenate([cos, cos], 1)
    seg_m = jnp.concatenate([-sin, z16], 1)
    seg_p = jnp.concatenate([z16, sin], 1)
    lay = lambda seg: jnp.concatenate([seg, z32, seg, z32], 1)
    return lay(seg_c), lay(seg_m), lay(seg_p)


def _layer_weights(l, g_mix_pre, g_mix_post, g_ffn_pre, g_ffn_post, w_in, g_cq, g_ckv, w_uq, w_ukv, mu_shift,
                   w0, w2, a0, a2, g2, k_k, k_a, r_k, lnx_w, lnx_b, w_out, w_up, w_down, g8):
    row = lambda x: x[l].reshape(1, -1)
    wi = w_in[l]
    w_kr = wi[:, Q_LORA + KV_LORA:MLA_COLS]
    z = jnp.zeros_like(w_kr)
    w_in_p = jnp.concatenate([wi[:, :Q_LORA + KV_LORA], w_kr, z, w_kr, z, wi[:, MLA_COLS:]], 1).astype(BF16)
    w_uq_p = jnp.pad(w_uq[l].reshape(Q_LORA, H_A, QK_DIM), ((0, 0), (0, 0), (0, HEAD_PAD - QK_DIM)))
    wkv = w_ukv[l].reshape(KV_LORA, H_A, NOPE_DIM + V_DIM)
    w_uk, w_uv = wkv[..., :NOPE_DIM], wkv[..., NOPE_DIM:]
    w_k_p = jnp.pad(w_uk, ((0, 0), (0, 0), (0, HEAD_PAD - NOPE_DIM)))
    w_uk_t = jnp.pad(jnp.transpose(w_uk, (1, 2, 0)), ((0, 0), (0, HEAD_PAD - NOPE_DIM), (0, 0)))
    eye = jnp.eye(H_A, dtype=F32)
    w_uv_p = (jnp.transpose(w_uv, (1, 0, 2))[:, :, None, :] * eye[:, None, :, None]).reshape(H_A, KV_LORA, ATT_DIM)
    zl = jnp.zeros((DECAY_LORA, RW_DIM), F32)
    return {
        "g_pre": row(g_mix_pre), "g_post": row(g_mix_post), "g_ffn_pre": row(g_ffn_pre),
        "g_ffn_post": row(g_ffn_post), "g_cq": row(g_cq), "g_ckv": row(g_ckv),
        "w_in": w_in_p, "w_uq": w_uq_p.reshape(Q_LORA, QK_PAD).astype(BF16),
        "w_k": w_k_p.reshape(KV_LORA, QK_PAD).astype(BF16), "w_v": w_uv.reshape(KV_LORA, ATT_DIM).astype(BF16),
        "w_uk_t": w_uk_t.astype(BF16), "w_uv_p": w_uv_p.astype(BF16),
        "mu": row(mu_shift), "w0": row(w0), "a0": row(a0), "k_k": row(k_k), "k_a": row(k_a), "r_k": row(r_k),
        "lnx_w": row(lnx_w), "lnx_b": row(lnx_b),
        "w2": jnp.concatenate([w2[l], zl], 0), "a2": jnp.concatenate([zl, a2[l]], 0), "g2": g2[l], "g8": g8,
        "w_out_att": w_out[l, :ATT_DIM].astype(BF16), "w_out_rw": w_out[l, ATT_DIM:].astype(BF16),
        "w_up": w_up[l].astype(BF16), "w_down": w_down[l].astype(BF16),
    }


def kernel(x_prompt, x_sample, cache_ckv, cache_krope, state_wkv, state_shift, page_table, meta_tokens,
           g_mix_pre, g_mix_post, g_ffn_pre, g_ffn_post, w_in, g_cq, g_ckv, w_uq, w_ukv, mu_shift, w0, w2,
           a0, a2, g2, k_k, k_a, r_k, lnx_w, lnx_b, w_out, w_up, w_down):
    bp, seq, _ = x_prompt.shape
    bd, dec_seq, _ = x_sample.shape
    depth = w_in.shape[0]
    assert dec_seq == 1, "the decode kernels handle one new token per sequence"
    seq_real = seq + N_META
    seq_pad = -(-seq_real // ATT_BLK) * ATT_BLK
    assert seq_pad % RW_STEP == 0 and (bp * seq_pad) % PROMPT_TILE == 0
    past_len = page_table.shape[1] * cache_ckv.shape[2]

    meta = jnp.broadcast_to(meta_tokens[None].astype(x_prompt.dtype), (bp, N_META, D_MODEL))
    tail = jnp.zeros((bp, seq_pad - seq_real, D_MODEL), x_prompt.dtype)
    h_p = jnp.concatenate([meta, x_prompt, tail], 1).reshape(bp * seq_pad, D_MODEL)
    h_s = x_sample.reshape(bd, D_MODEL)
    tab_p = _rope_tables(jnp.tile(jnp.arange(seq_pad), bp))
    tab_s = _rope_tables(jnp.full((bd,), past_len, jnp.int32))
    hi = lax.broadcasted_iota(jnp.int32, (RW_DIM, RW_DIM), 0) // N_R
    hj = lax.broadcasted_iota(jnp.int32, (RW_DIM, RW_DIM), 1) // N_R
    g8 = (hi == hj).astype(BF16)

    outs = [[] for _ in range(8)]
    for l in range(depth):
        lw = _layer_weights(l, g_mix_pre, g_mix_post, g_ffn_pre, g_ffn_post, w_in, g_cq, g_ckv, w_uq, w_ukv,
                            mu_shift, w0, w2, a0, a2, g2, k_k, k_a, r_k, lnx_w, lnx_b, w_out, w_up, w_down, g8)
        q, k, v, ckv, kr, rw = _proj_call(h_p, lw, tab_p, PROMPT_TILE)
        o_att = _attn_call(q, k, v, bp, seq_pad)
        o_rw, st = _rwkv_prompt_call(rw, lw, bp, seq_pad, seq_real)
        h_p = _finish_call(h_p, o_att, o_rw, lw, PROMPT_TILE)
        outs[0].append(ckv.reshape(bp, seq_pad, KV_LORA)[:, :seq_real])
        outs[1].append(kr.reshape(bp, seq_pad, ROPE_DIM)[:, :seq_real])
        outs[2].append(jnp.swapaxes(st, -1, -2))
        outs[3].append(rw.reshape(bp, seq_pad, RW_COLS)[:, seq_real - 1])
        q, _, _, ckv, kr, rw = _proj_call(h_s, lw, tab_s, bd)
        q_lat = _qlat_call(q, lw["w_uk_t"]).reshape(bd, H_A, KV_LORA)
        q_rope = q.reshape(bd, H_A, HEAD_PAD)[:, :, NOPE_DIM:QK_DIM].astype(F32)
        o_lat = _decode_attn_call(q_lat, q_rope, ckv.reshape(bd, 1, KV_LORA), kr.reshape(bd, 1, ROPE_DIM),
                                  cache_ckv, cache_krope, page_table, l)
        o_att = _ouv_call(o_lat.reshape(bd, H_A * KV_LORA), lw["w_uv_p"])
        o_rw, s_new = _rwkv_step_call(rw, state_shift[l], state_wkv, l, lw)
        h_s = _finish_call(h_s, o_att, o_rw, lw, bd)
        outs[4].append(ckv.reshape(bd, 1, KV_LORA))
        outs[5].append(kr.reshape(bd, 1, ROPE_DIM))
        outs[6].append(s_new)
        outs[7].append(rw)

    y_prompt = h_p.reshape(bp, seq_pad, D_MODEL)[:, N_META:seq_real]
    y_sample = h_s.reshape(bd, 1, D_MODEL)
    return (y_prompt, y_sample) + tuple(jnp.stack(o) for o in outs)
```

```python
import functools
import math

import jax
import jax.numpy as jnp
from jax import lax
from jax.experimental import pallas as pl
from jax.experimental.pallas import tpu as pltpu

F32 = jnp.float32
BF16 = jnp.bfloat16

D_MODEL = 1024
N_META = 16
V_DIM = 64
NOPE_DIM = 64
ROPE_DIM = 32
QK_DIM = NOPE_DIM + ROPE_DIM
H_A = 8
Q_LORA = 384
KV_LORA = 256
ROPE_BASE = 10000.0
ATTN_SCALE = QK_DIM ** -0.5
N_R = 64
H_R = 8
RW_DIM = H_R * N_R
DECAY_LORA = 64
AAA_LORA = 64
GATE_LORA = 128
LNX_EPS = 64e-5
ATT_DIM = H_A * V_DIM
MLA_COLS = Q_LORA + KV_LORA + ROPE_DIM
RW_COLS = 3 * RW_DIM + DECAY_LORA + AAA_LORA + GATE_LORA
D_FF = 4 * D_MODEL
NORM_EPS = 1e-6
PAGE_SIZE = 128

LANES = 128
VMEM_LIMIT = 48 * 1024 * 1024
HEAD_PAD = LANES
QK_PAD = H_A * HEAD_PAD
IN_PAD = Q_LORA + KV_LORA + LANES + RW_COLS
ATT_BLK = 384
RW_CHUNK = 64
RW_STEP = 384
RW_GROUP = 3
PROMPT_TILE = 512
PAGES_PER_STEP = 32

_NT = (((1,), (1,)), ((), ()))
_TN = (((0,), (0,)), ((), ()))
_NN = (((1,), (0,)), ((), ()))


def _dg(a, b, dims=_NN):
    return lax.dot_general(a, b, dims, preferred_element_type=F32)


def _dot(a, b, dims=_NN):
    return _dg(a.astype(BF16), b.astype(BF16), dims)


def _split(x):
    hi = x.astype(BF16)
    lo = (x - hi.astype(F32)).astype(BF16)
    return hi, lo


def _dot3(a, b, dims=_NN):
    ah, al = _split(a)
    bh, bl = _split(b)
    return _dg(ah, bh, dims) + (_dg(ah, bl, dims) + _dg(al, bh, dims))


def _rms(x, g):
    ms = jnp.mean(x * x, axis=-1, keepdims=True)
    return x * lax.rsqrt(ms + NORM_EPS) * g


def _const_spec(shape):
    nd = len(shape)
    return pl.BlockSpec(shape, lambda *_: (0,) * nd, pipeline_mode=pl.Buffered(1))


def _params(sem):
    return pltpu.CompilerParams(dimension_semantics=sem, vmem_limit_bytes=VMEM_LIMIT)


def _rope_chunk(x, tc, tm, tp):
    return x * tc + pltpu.roll(x, LANES - 16, 1) * tm + pltpu.roll(x, 16, 1) * tp


def _proj_kernel(h_ref, gpre_ref, win_ref, gcq_ref, gckv_ref, wuq_ref, wk_ref, wv_ref,
                 tc_ref, tm_ref, tp_ref, q_ref, k_ref, v_ref, ckv_ref, kr_ref, rw_ref):
    xn = _rms(h_ref[...], gpre_ref[...]).astype(BF16)
    tc, tm, tp = tc_ref[...], tm_ref[...], tp_ref[...]
    lane = lax.broadcasted_iota(jnp.int32, tc.shape, 1)

    cq = _rms(_dg(xn, win_ref[:, 0:Q_LORA]), gcq_ref[...])
    qf = _dot(cq, wuq_ref[...])
    for h in range(H_A):
        x = qf[:, h * HEAD_PAD:(h + 1) * HEAD_PAD]
        y = jnp.where(lane < NOPE_DIM, x, _rope_chunk(x, tc, tm, tp)) * ATTN_SCALE
        q_ref[:, h * HEAD_PAD:(h + 1) * HEAD_PAD] = y.astype(BF16)

    ckv = _rms(_dg(xn, win_ref[:, Q_LORA:Q_LORA + KV_LORA]), gckv_ref[...])
    ckv_ref[...] = ckv
    cb = ckv.astype(BF16)
    v_ref[...] = _dg(cb, wv_ref[...]).astype(BF16)
    kf = _dg(cb, wk_ref[...])
    kr0 = Q_LORA + KV_LORA
    ykr = _rope_chunk(_dg(xn, win_ref[:, kr0:kr0 + LANES]), tc, tm, tp)
    kr_ref[...] = ykr[:, 0:ROPE_DIM]
    kadd = jnp.where(lane >= NOPE_DIM, ykr, 0.0)
    for h in range(H_A):
        k_ref[:, h * HEAD_PAD:(h + 1) * HEAD_PAD] = (kf[:, h * HEAD_PAD:(h + 1) * HEAD_PAD] + kadd).astype(BF16)

    rw_ref[...] = _dg(xn, win_ref[:, kr0 + LANES:IN_PAD])


def _proj_call(h, lw, tables, tile):
    rows = h.shape[0]
    row = lambda n: pl.BlockSpec((tile, n), lambda i: (i, 0))
    return pl.pallas_call(
        _proj_kernel,
        grid=(rows // tile,),
        in_specs=[row(D_MODEL), _const_spec((1, D_MODEL)), _const_spec((D_MODEL, IN_PAD)),
                  _const_spec((1, Q_LORA)), _const_spec((1, KV_LORA)), _const_spec((Q_LORA, QK_PAD)),
                  _const_spec((KV_LORA, QK_PAD)), _const_spec((KV_LORA, ATT_DIM)),
                  row(LANES), row(LANES), row(LANES)],
        out_specs=[row(QK_PAD), row(QK_PAD), row(ATT_DIM), row(KV_LORA), row(ROPE_DIM), row(RW_COLS)],
        out_shape=[jax.ShapeDtypeStruct((rows, QK_PAD), BF16), jax.ShapeDtypeStruct((rows, QK_PAD), BF16),
                   jax.ShapeDtypeStruct((rows, ATT_DIM), BF16), jax.ShapeDtypeStruct((rows, KV_LORA), F32),
                   jax.ShapeDtypeStruct((rows, ROPE_DIM), F32), jax.ShapeDtypeStruct((rows, RW_COLS), F32)],
        compiler_params=_params(("parallel",)),
    )(h, lw["g_pre"], lw["w_in"], lw["g_cq"], lw["g_ckv"], lw["w_uq"], lw["w_k"], lw["w_v"], *tables)


def _attn_kernel(q_ref, k_ref, v_ref, o_ref):
    qi = pl.program_id(2)
    q = q_ref[...]
    row = lax.broadcasted_iota(jnp.int32, (ATT_BLK, ATT_BLK), 0)
    col = lax.broadcasted_iota(jnp.int32, (ATT_BLK, ATT_BLK), 1)

    def step(j, carry, masked):
        start = pl.multiple_of(j * ATT_BLK, ATT_BLK)
        ks = k_ref[pl.ds(start, ATT_BLK), :]
        vs = v_ref[pl.ds(start, ATT_BLK), :]
        out = []
        for hh in range(2):
            m, l, acc = carry[hh]
            s = _dg(q[:, hh * HEAD_PAD:(hh + 1) * HEAD_PAD], ks[:, hh * HEAD_PAD:(hh + 1) * HEAD_PAD], _NT)
            if masked:
                s = jnp.where(col <= row, s, -jnp.inf)
            m_new = jnp.maximum(m, jnp.max(s, axis=-1, keepdims=True))
            alpha = jnp.exp(m - m_new)
            p = jnp.exp(s - m_new)
            l = alpha * l + jnp.sum(p, axis=-1, keepdims=True)
            acc = alpha * acc + _dg(p.astype(BF16), vs)
            out.append((m_new, l, acc))
        return tuple(out)

    one = (jnp.full((ATT_BLK, 1), -jnp.inf, F32), jnp.zeros((ATT_BLK, 1), F32),
           jnp.zeros((ATT_BLK, 2 * V_DIM), F32))
    carry = lax.fori_loop(0, qi, lambda j, c: step(j, c, False), (one, one))
    (_, l0, a0), (_, l1, a1) = step(qi, carry, True)
    lane = lax.broadcasted_iota(jnp.int32, (ATT_BLK, 2 * V_DIM), 1)
    o_ref[...] = jnp.where(lane < V_DIM, a0 / l0, a1 / l1).astype(BF16)


def _attn_call(q, k, v, batch, seq_pad):
    nq = seq_pad // ATT_BLK
    return pl.pallas_call(
        _attn_kernel,
        grid=(batch, H_A // 2, nq),
        in_specs=[pl.BlockSpec((ATT_BLK, 2 * HEAD_PAD), lambda b, hp, qi: (b * nq + qi, hp)),
                  pl.BlockSpec((seq_pad, 2 * HEAD_PAD), lambda b, hp, qi: (b, hp)),
                  pl.BlockSpec((seq_pad, 2 * V_DIM), lambda b, hp, qi: (b, hp))],
        out_specs=pl.BlockSpec((ATT_BLK, 2 * V_DIM), lambda b, hp, qi: (b * nq + qi, hp)),
        out_shape=jax.ShapeDtypeStruct((batch * seq_pad, ATT_DIM), BF16),
        compiler_params=_params(("parallel", "parallel", "arbitrary")),
    )(q, k, v)


def _rwkv_pre(x, prev, mu, w0, w2p, a0, a2p, g2, k_k, k_a, g8):
    xs = x + (prev - x) * mu
    r = xs[:, 0:RW_DIM]
    k = xs[:, RW_DIM:2 * RW_DIM]
    v = xs[:, 2 * RW_DIM:3 * RW_DIM]
    wa = xs[:, 3 * RW_DIM:3 * RW_DIM + LANES]
    gd = xs[:, 3 * RW_DIM + LANES:RW_COLS]
    z = w0 + _dot3(jnp.tanh(wa), w2p)
    lw = -math.exp(-0.5) * jax.nn.sigmoid(z)
    a = jax.nn.sigmoid(a0 + _dot(wa, a2p))
    g = _dot(jax.nn.sigmoid(gd), g2)
    kkr = k * k_k
    kk = kkr * lax.rsqrt(jnp.maximum(_dot(kkr * kkr, g8), 1e-24))
    k2 = k * (1.0 + (a - 1.0) * k_a)
    return r, k2, v, kk, a, lw, g


def _rwkv_post(y, r, k2, v, g, r_k, lnx_w, lnx_b, g8):
    mean = _dot(y, g8) * (1.0 / N_R)
    d = y - mean
    var = _dot(d * d, g8) * (1.0 / N_R)
    yn = d * lax.rsqrt(var + LNX_EPS) * lnx_w + lnx_b
    bonus = _dot(r * k2 * r_k, g8) * v
    return (yn + bonus) * g


def _rwkv_prompt_kernel(seq_real, rw_ref, mu_ref, w0_ref, w2_ref, a0_ref, a2_ref, g2_ref, kk_ref, ka_ref,
                        rk_ref, lnw_ref, lnb_ref, g8_ref, o_ref, st_ref,
                        st_scr, last_scr, rt_scr, at_scr, bt_scr, kt_scr, bw_scr, kw_scr, v_scr, wc_scr, y_scr):
    t = pl.program_id(1)
    C = RW_CHUNK

    @pl.when(t == 0)
    def _():
        st_scr[...] = jnp.zeros_like(st_scr)
        last_scr[...] = jnp.zeros_like(last_scr)

    x = rw_ref[...]
    rows = lax.broadcasted_iota(jnp.int32, (RW_STEP, 1), 0)
    prev = jnp.where(rows == 0, last_scr[...], pltpu.roll(x, 1, 0))
    last_scr[...] = x[RW_STEP - 1:RW_STEP, :]
    g8 = g8_ref[...]
    r, k2, v, kk, a, lw, g = _rwkv_pre(x, prev, mu_ref[...], w0_ref[...], w2_ref[...], a0_ref[...],
                                       a2_ref[...], g2_ref[...], kk_ref[...], ka_ref[...], g8)
    valid = (t * RW_STEP + rows) < seq_real
    lw = jnp.where(valid, lw, 0.0)
    kk = jnp.where(valid, kk, 0.0)
    k2m = jnp.where(valid, k2, 0.0)

    ri = lax.broadcasted_iota(jnp.int32, (LANES, LANES), 0)
    ci = lax.broadcasted_iota(jnp.int32, (LANES, LANES), 1)
    tri = jnp.where(((ri // C) == (ci // C)) & (ci <= ri), 1.0, 0.0).astype(BF16)
    lhi, llo = _split(lw)
    cum = jnp.concatenate([_dg(tri, lhi[i:i + LANES]) + _dg(tri, llo[i:i + LANES])
                           for i in range(0, RW_STEP, LANES)], 0)
    tot = jnp.concatenate([jnp.broadcast_to(cum[i + C - 1:i + C], (C, RW_DIM))
                           for i in range(0, RW_STEP, C)], 0)
    einv = jnp.exp(-cum)
    etail = jnp.exp(tot - cum)
    kb = kk * a
    rt_scr[...] = r * jnp.exp(cum)
    at_scr[...] = -kk * jnp.exp(cum - lw)
    bt_scr[...] = kb * einv
    kt_scr[...] = k2m * einv
    bw_scr[...] = kb * etail
    kw_scr[...] = k2m * etail
    v_scr[...] = v
    wc_scr[...] = jnp.exp(tot)

    r64 = lax.broadcasted_iota(jnp.int32, (C, C), 0)
    c64 = lax.broadcasted_iota(jnp.int32, (C, C), 1)
    strict = c64 < r64
    eye = c64 == r64
    r128 = lax.broadcasted_iota(jnp.int32, (C, 2 * C), 0)
    c128 = lax.broadcasted_iota(jnp.int32, (C, 2 * C), 1)
    incl2 = jnp.where(c128 >= C, c128 - C, c128) <= r128
    zeros64 = jnp.zeros((C, C), F32)

    def group(gi, carry):
        keys = [(cc, h) for cc in range(RW_GROUP) for h in range(H_R)]

        def rows(cc):
            return pl.ds(pl.multiple_of((gi * RW_GROUP + cc) * C, C), C)

        def ld(scr):
            return {(cc, h): scr[rows(cc), h * N_R:(h + 1) * N_R] for cc, h in keys}

        at, rt, v_ = ld(at_scr), ld(rt_scr), ld(v_scr)
        bt, kt = ld(bt_scr), ld(kt_scr)
        bk = {key: jnp.concatenate([bt[key], kt[key]], 0).astype(BF16) for key in keys}
        sc = {key: _dg(jnp.concatenate([at[key], rt[key]], 0).astype(BF16), bk[key], _NT) for key in keys}
        npow = {key: jnp.where(strict, sc[key][0:C, 0:C], 0.0).astype(BF16) for key in keys}
        a_ak = {key: jnp.where(strict, sc[key][0:C, C:2 * C], 0.0).astype(BF16) for key in keys}
        a_rbk = {key: jnp.where(incl2, sc[key][C:2 * C, :], 0.0).astype(BF16) for key in keys}
        vb = {key: v_[key].astype(BF16) for key in keys}
        xx = {key: jnp.concatenate([at[key], _dg(a_ak[key], vb[key])], 1) for key in keys}
        for j in range(6):
            xx = {key: xx[key] + _dg(npow[key], xx[key].astype(BF16)) for key in keys}
            if j < 5:
                npow = {key: _dg(npow[key], npow[key]).astype(BF16) for key in keys}
        zz = {key: jnp.concatenate([xx[key], jnp.concatenate([zeros64, v_[key]], 1)], 0).astype(BF16)
              for key in keys}
        yz = {key: _dg(a_rbk[key], zz[key]) for key in keys}
        bw, kw = ld(bw_scr), ld(kw_scr)
        bkw = {key: jnp.concatenate([bw[key], kw[key]], 0).astype(BF16) for key in keys}
        mz = {key: _dg(bkw[key], zz[key], _TN) for key in keys}
        wc = ld(wc_scr)
        st = [st_scr[h] for h in range(H_R)]
        for cc in range(RW_GROUP):
            stb = [s.astype(BF16) for s in st]
            for h in range(H_R):
                key = (cc, h)
                y_scr[rows(cc), h * N_R:(h + 1) * N_R] = (
                    _dg((rt[key] + yz[key][:, 0:C]).astype(BF16), stb[h]) + yz[key][:, C:2 * C])
                m = jnp.where(eye, wc[key], 0.0) + mz[key][:, 0:C]
                st[h] = _dg(m.astype(BF16), stb[h]) + mz[key][:, C:2 * C]
        for h in range(H_R):
            st_scr[h] = st[h]
        return carry

    lax.fori_loop(0, RW_STEP // C // RW_GROUP, group, 0)

    out = _rwkv_post(y_scr[...], r, k2, v, g, rk_ref[...], lnw_ref[...], lnb_ref[...], g8)
    o_ref[...] = out.astype(BF16)

    @pl.when(t == pl.num_programs(1) - 1)
    def _():
        st_ref[0] = st_scr[...]


def _rwkv_prompt_call(rw, lw, batch, seq_pad, seq_real):
    nt = seq_pad // RW_STEP
    vec = lambda n: _const_spec((1, n))
    big = pltpu.VMEM((RW_STEP, RW_DIM), F32)
    return pl.pallas_call(
        functools.partial(_rwkv_prompt_kernel, seq_real),
        grid=(batch, nt),
        in_specs=[pl.BlockSpec((RW_STEP, RW_COLS), lambda b, t: (b * nt + t, 0)),
                  vec(RW_COLS), vec(RW_DIM), _const_spec((LANES, RW_DIM)), vec(RW_DIM),
                  _const_spec((LANES, RW_DIM)), _const_spec((GATE_LORA, RW_DIM)), vec(RW_DIM), vec(RW_DIM),
                  vec(RW_DIM), vec(RW_DIM), vec(RW_DIM), _const_spec((RW_DIM, RW_DIM))],
        out_specs=[pl.BlockSpec((RW_STEP, RW_DIM), lambda b, t: (b * nt + t, 0)),
                   pl.BlockSpec((1, H_R, N_R, N_R), lambda b, t: (b, 0, 0, 0))],
        out_shape=[jax.ShapeDtypeStruct((batch * seq_pad, RW_DIM), BF16),
                   jax.ShapeDtypeStruct((batch, H_R, N_R, N_R), F32)],
        scratch_shapes=[pltpu.VMEM((H_R, N_R, N_R), F32), pltpu.VMEM((1, RW_COLS), F32),
                        big, big, big, big, big, big, big, big, big],
        compiler_params=_params(("arbitrary", "arbitrary")),
    )(rw, lw["mu"], lw["w0"], lw["w2"], lw["a0"], lw["a2"], lw["g2"], lw["k_k"], lw["k_a"],
      lw["r_k"], lw["lnx_w"], lw["lnx_b"], lw["g8"])


def _rwkv_step_kernel(rw_ref, sh_ref, s_ref, mu_ref, w0_ref, w2_ref, a0_ref, a2_ref, g2_ref, kk_ref, ka_ref,
                      rk_ref, lnw_ref, lnb_ref, g8_ref, o_ref, so_ref,
                      r_scr, k_scr, v_scr, g_scr, at_scr, bt_scr, wt_scr, kt_scr, vt_scr, rt_scr, yt_scr):
    h = pl.program_id(0)

    @pl.when(h == 0)
    def _():
        r, k2, v, kk, a, lw, g = _rwkv_pre(rw_ref[...], sh_ref[...], mu_ref[...], w0_ref[...], w2_ref[...],
                                           a0_ref[...], a2_ref[...], g2_ref[...], kk_ref[...], ka_ref[...],
                                           g8_ref[...])
        r_scr[...], k_scr[...], v_scr[...], g_scr[...] = r, k2, v, g
        at_scr[...] = (-kk).T
        bt_scr[...] = (kk * a).T
        wt_scr[...] = jnp.exp(lw).T
        kt_scr[...] = k2.T
        vt_scr[...] = v.T
        rt_scr[...] = r.T

    hrows = pl.ds(pl.multiple_of(h * N_R, N_R), N_R)
    a_h, b_h, w_h, k_h, r_h = at_scr[hrows, :], bt_scr[hrows, :], wt_scr[hrows, :], kt_scr[hrows, :], rt_scr[hrows, :]

    def vblock(vb, carry):
        v0 = pl.multiple_of(vb * 8, 8)
        vv = vt_scr[pl.ds(h * N_R + v0, 8), :]
        ys = []
        for i in range(8):
            s = s_ref[v0 + i]
            sa = jnp.sum(s * a_h, axis=0, keepdims=True)
            s2 = s * w_h + sa * b_h + vv[i:i + 1, :] * k_h
            so_ref[v0 + i] = s2
            ys.append(jnp.sum(s2 * r_h, axis=0, keepdims=True))
        yt_scr[pl.ds(h * N_R + v0, 8), :] = jnp.concatenate(ys, 0)
        return carry

    lax.fori_loop(0, N_R // 8, vblock, 0)

    @pl.when(h == pl.num_programs(0) - 1)
    def _():
        o_ref[...] = _rwkv_post(yt_scr[...].T, r_scr[...], k_scr[...], v_scr[...], g_scr[...], rk_ref[...],
                                lnw_ref[...], lnb_ref[...], g8_ref[...])


def _rwkv_step_call(rw, shift, state_t, layer, lw):
    nb = rw.shape[0]
    vec = lambda n: _const_spec((1, n))
    rows = pltpu.VMEM((nb, RW_DIM), F32)
    cols = pltpu.VMEM((RW_DIM, nb), F32)
    return pl.pallas_call(
        _rwkv_step_kernel,
        grid=(H_R,),
        in_specs=[_const_spec((nb, RW_COLS)), _const_spec((nb, RW_COLS)),
                  pl.BlockSpec((None, None, N_R, N_R, nb), lambda h: (layer, h, 0, 0, 0)),
                  vec(RW_COLS), vec(RW_DIM), _const_spec((LANES, RW_DIM)), vec(RW_DIM),
                  _const_spec((LANES, RW_DIM)), _const_spec((GATE_LORA, RW_DIM)), vec(RW_DIM), vec(RW_DIM),
                  vec(RW_DIM), vec(RW_DIM), vec(RW_DIM), _const_spec((RW_DIM, RW_DIM))],
        out_specs=[pl.BlockSpec((nb, RW_DIM), lambda h: (0, 0)),
                   pl.BlockSpec((None, N_R, N_R, nb), lambda h: (h, 0, 0, 0))],
        out_shape=[jax.ShapeDtypeStruct((nb, RW_DIM), F32),
                   jax.ShapeDtypeStruct((H_R, N_R, N_R, nb), F32)],
        scratch_shapes=[rows, rows, rows, rows, cols, cols, cols, cols, cols, cols, cols],
        compiler_params=_params(("arbitrary",)),
    )(rw, shift, state_t, lw["mu"], lw["w0"], lw["w2"], lw["a0"], lw["a2"], lw["g2"], lw["k_k"],
      lw["k_a"], lw["r_k"], lw["lnx_w"], lw["lnx_b"], lw["g8"])


def _finish_kernel(h_ref, oa_ref, orw_ref, woa_ref, wor_ref, gpost_ref, gfpre_ref, gfpost_ref,
                   wup_ref, wdn_ref, out_ref):
    o = _dot(oa_ref[...], woa_ref[...]) + _dot(orw_ref[...], wor_ref[...])
    h1 = h_ref[...] + _rms(o, gpost_ref[...])
    xn = _rms(h1, gfpre_ref[...]).astype(BF16)
    acc = jnp.zeros(h1.shape, F32)
    for c in range(D_FF // D_MODEL):
        cs = slice(c * D_MODEL, (c + 1) * D_MODEL)
        u = jnp.square(jnp.maximum(_dg(xn, wup_ref[:, cs]), 0.0))
        acc = acc + _dg(u.astype(BF16), wdn_ref[cs, :])
    out_ref[...] = h1 + _rms(acc, gfpost_ref[...])


def _finish_call(h, o_att, o_rw, lw, tile):
    rows = h.shape[0]
    row = lambda n: pl.BlockSpec((tile, n), lambda i: (i, 0))
    vec = _const_spec((1, D_MODEL))
    return pl.pallas_call(
        _finish_kernel,
        grid=(rows // tile,),
        in_specs=[row(D_MODEL), row(ATT_DIM), row(RW_DIM), _const_spec((ATT_DIM, D_MODEL)),
                  _const_spec((RW_DIM, D_MODEL)), vec, vec, vec,
                  _const_spec((D_MODEL, D_FF)), _const_spec((D_FF, D_MODEL))],
        out_specs=row(D_MODEL),
        out_shape=jax.ShapeDtypeStruct((rows, D_MODEL), F32),
        compiler_params=_params(("parallel",)),
    )(h, o_att, o_rw, lw["w_out_att"], lw["w_out_rw"], lw["g_post"], lw["g_ffn_pre"], lw["g_ffn_post"],
      lw["w_up"], lw["w_down"])


def _qlat_kernel(q_ref, wuk_ref, o_ref):
    q = q_ref[...]
    for h in range(H_A):
        o_ref[:, h * KV_LORA:(h + 1) * KV_LORA] = _dg(q[:, h * HEAD_PAD:(h + 1) * HEAD_PAD], wuk_ref[h])


def _qlat_call(q, w_uk):
    nb = q.shape[0]
    return pl.pallas_call(
        _qlat_kernel,
        out_shape=jax.ShapeDtypeStruct((nb, H_A * KV_LORA), F32),
    )(q, w_uk)


def _ouv_kernel(ol_ref, wuv_ref, o_ref):
    ol = ol_ref[...]
    acc = jnp.zeros(o_ref.shape, F32)
    for h in range(H_A):
        acc = acc + _dot(ol[:, h * KV_LORA:(h + 1) * KV_LORA], wuv_ref[h])
    o_ref[...] = acc


def _ouv_call(o_lat, w_uv):
    nb = o_lat.shape[0]
    return pl.pallas_call(
        _ouv_kernel,
        out_shape=jax.ShapeDtypeStruct((nb, ATT_DIM), F32),
    )(o_lat, w_uv)


def _decode_attn_kernel(pt_ref, ql_ref, qr_ref, cn_ref, kn_ref, *rest):
    n = PAGES_PER_STEP
    ckv_refs, kr_refs = rest[0:n], rest[n:2 * n]
    o_ref, m_scr, l_scr, acc_scr = rest[2 * n:]
    j = pl.program_id(1)

    @pl.when(j == 0)
    def _():
        m_scr[...] = jnp.full_like(m_scr, -jnp.inf)
        l_scr[...] = jnp.zeros_like(l_scr)
        acc_scr[...] = jnp.zeros_like(acc_scr)

    ql = ql_ref[0]
    qr = qr_ref[0]
    qlb, qrb = ql.astype(BF16), qr.astype(BF16)
    scores, pages = [], []
    for i in range(n):
        cb = ckv_refs[i][...].astype(BF16)
        kb = kr_refs[i][...].astype(BF16)
        scores.append(_dg(qlb, cb, _NT) + _dg(qrb, kb))
        pages.append(cb)
    s = jnp.concatenate(scores, 1)
    m_old = m_scr[...]
    m_new = jnp.maximum(m_old, jnp.max(s, axis=-1, keepdims=True))
    alpha = jnp.exp(m_old - m_new)
    p = jnp.exp(s - m_new)
    pb = p.astype(BF16)
    pv = _dg(pb[:, 0:PAGE_SIZE], pages[0])
    for i in range(1, n):
        pv = pv + _dg(pb[:, i * PAGE_SIZE:(i + 1) * PAGE_SIZE], pages[i])
    l_new = alpha * l_scr[...] + jnp.sum(p, axis=-1, keepdims=True)
    acc_new = alpha * acc_scr[...] + pv
    m_scr[...] = m_new
    l_scr[...] = l_new
    acc_scr[...] = acc_new

    @pl.when(j == pl.num_programs(1) - 1)
    def _():
        cn = cn_ref[0]
        kn = kn_ref[0]
        s_n = jnp.sum(ql * cn, axis=-1, keepdims=True) + jnp.sum(qr * kn, axis=-1, keepdims=True)
        m2 = jnp.maximum(m_new, s_n)
        a2 = jnp.exp(m_new - m2)
        pn = jnp.exp(s_n - m2)
        o_ref[0] = (a2 * acc_new + pn * cn) / (a2 * l_new + pn)


def _decode_attn_call(q_lat, q_rope, ckv_new, kr_new, cache_ckv, cache_krope, page_table, layer):
    nb, n_pages = page_table.shape
    n = PAGES_PER_STEP

    def page_spec(shape, i):
        return pl.BlockSpec((None, None) + shape, lambda b, j, pt: (layer, pt[b, j * n + i], 0, 0))

    per_b = lambda shape: pl.BlockSpec((1,) + shape, lambda b, j, pt: (b, 0, 0))
    grid_spec = pltpu.PrefetchScalarGridSpec(
        num_scalar_prefetch=1,
        grid=(nb, n_pages // n),
        in_specs=[per_b((H_A, KV_LORA)), per_b((H_A, ROPE_DIM)), per_b((1, KV_LORA)), per_b((1, ROPE_DIM))]
                 + [page_spec((PAGE_SIZE, KV_LORA), i) for i in range(n)]
                 + [page_spec((ROPE_DIM, PAGE_SIZE), i) for i in range(n)],
        out_specs=per_b((H_A, KV_LORA)),
        scratch_shapes=[pltpu.VMEM((H_A, 1), F32), pltpu.VMEM((H_A, 1), F32), pltpu.VMEM((H_A, KV_LORA), F32)],
    )
    return pl.pallas_call(
        _decode_attn_kernel,
        grid_spec=grid_spec,
        out_shape=jax.ShapeDtypeStruct((nb, H_A, KV_LORA), F32),
        compiler_params=_params(("parallel", "arbitrary")),
    )(page_table, q_lat, q_rope, ckv_new, kr_new, *([cache_ckv] * n), *([cache_krope] * n))


def _rope_tables(pos):
    half = ROPE_DIM // 2
    inv = ROPE_BASE ** (-jnp.arange(half, dtype=F32) / half)
    ang = pos.astype(F32)[:, None] * inv[None, :]
    cos, sin = jnp.cos(ang), jnp.sin(ang)
    z16, z32 = jnp.zeros_like(cos), jnp.zeros((pos.shape[0], ROPE_DIM), F32)
    seg_c = jnp.concatenate([cos, cos], 1)
    seg_m = jnp.concatenate([-sin, z16], 1)
    seg_p = jnp.concatenate([z16, sin], 1)
    lay = lambda seg: jnp.concatenate([seg, z32, seg, z32], 1)
    return lay(seg_c), lay(seg_m), lay(seg_p)


def _layer_weights(l, g_mix_pre, g_mix_post, g_ffn_pre, g_ffn_post, w_in, g_cq, g_ckv, w_uq, w_ukv, mu_shift,
                   w0, w2, a0, a2, g2, k_k, k_a, r_k, lnx_w, lnx_b, w_out, w_up, w_down, g8):
    row = lambda x: x[l].reshape(1, -1)
    wi = w_in[l]
    w_kr = wi[:, Q_LORA + KV_LORA:MLA_COLS]
    z = jnp.zeros_like(w_kr)
    w_in_p = jnp.concatenate([wi[:, :Q_LORA + KV_LORA], w_kr, z, w_kr, z, wi[:, MLA_COLS:]], 1).astype(BF16)
    w_uq_p = jnp.pad(w_uq[l].reshape(Q_LORA, H_A, QK_DIM), ((0, 0), (0, 0), (0, HEAD_PAD - QK_DIM)))
    wkv = w_ukv[l].reshape(KV_LORA, H_A, NOPE_DIM + V_DIM)
    w_uk, w_uv = wkv[..., :NOPE_DIM], wkv[..., NOPE_DIM:]
    w_k_p = jnp.pad(w_uk, ((0, 0), (0, 0), (0, HEAD_PAD - NOPE_DIM)))
    w_uk_t = jnp.pad(jnp.transpose(w_uk, (1, 2, 0)), ((0, 0), (0, HEAD_PAD - NOPE_DIM), (0, 0)))
    eye = jnp.eye(H_A, dtype=F32)
    w_uv_p = (jnp.transpose(w_uv, (1, 0, 2))[:, :, None, :] * eye[:, None, :, None]).reshape(H_A, KV_LORA, ATT_DIM)
    zl = jnp.zeros((DECAY_LORA, RW_DIM), F32)
    return {
        "g_pre": row(g_mix_pre), "g_post": row(g_mix_post), "g_ffn_pre": row(g_ffn_pre),
        "g_ffn_post": row(g_ffn_post), "g_cq": row(g_cq), "g_ckv": row(g_ckv),
        "w_in": w_in_p, "w_uq": w_uq_p.reshape(Q_LORA, QK_PAD).astype(BF16),
        "w_k": w_k_p.reshape(KV_LORA, QK_PAD).astype(BF16), "w_v": w_uv.reshape(KV_LORA, ATT_DIM).astype(BF16),
        "w_uk_t": w_uk_t.astype(BF16), "w_uv_p": w_uv_p.astype(BF16),
        "mu": row(mu_shift), "w0": row(w0), "a0": row(a0), "k_k": row(k_k), "k_a": row(k_a), "r_k": row(r_k),
        "lnx_w": row(lnx_w), "lnx_b": row(lnx_b),
        "w2": jnp.concatenate([w2[l], zl], 0), "a2": jnp.concatenate([zl, a2[l]], 0), "g2": g2[l], "g8": g8,
        "w_out_att": w_out[l, :ATT_DIM].astype(BF16), "w_out_rw": w_out[l, ATT_DIM:].astype(BF16),
        "w_up": w_up[l].astype(BF16), "w_down": w_down[l].astype(BF16),
    }


def kernel(x_prompt, x_sample, cache_ckv, cache_krope, state_wkv, state_shift, page_table, meta_tokens,
           g_mix_pre, g_mix_post, g_ffn_pre, g_ffn_post, w_in, g_cq, g_ckv, w_uq, w_ukv, mu_shift, w0, w2,
           a0, a2, g2, k_k, k_a, r_k, lnx_w, lnx_b, w_out, w_up, w_down):
    bp, seq, _ = x_prompt.shape
    bd, dec_seq, _ = x_sample.shape
    depth = w_in.shape[0]
    assert dec_seq == 1, "the decode kernels handle one new token per sequence"
    seq_real = seq + N_META
    seq_pad = -(-seq_real // ATT_BLK) * ATT_BLK
    assert seq_pad % RW_STEP == 0 and (bp * seq_pad) % PROMPT_TILE == 0
    past_len = page_table.shape[1] * cache_ckv.shape[2]

    meta = jnp.broadcast_to(meta_tokens[None].astype(x_prompt.dtype), (bp, N_META, D_MODEL))
    tail = jnp.zeros((bp, seq_pad - seq_real, D_MODEL), x_prompt.dtype)
    h_p = jnp.concatenate([meta, x_prompt, tail], 1).reshape(bp * seq_pad, D_MODEL)
    h_s = x_sample.reshape(bd, D_MODEL)
    tab_p = _rope_tables(jnp.tile(jnp.arange(seq_pad), bp))
    tab_s = _rope_tables(jnp.full((bd,), past_len, jnp.int32))
    hi = lax.broadcasted_iota(jnp.int32, (RW_DIM, RW_DIM), 0) // N_R
    hj = lax.broadcasted_iota(jnp.int32, (RW_DIM, RW_DIM), 1) // N_R
    g8 = (hi == hj).astype(BF16)
    cache_krope_t = jnp.swapaxes(cache_krope, 2, 3)
    state_t = jnp.transpose(state_wkv, (0, 2, 3, 4, 1))

    outs = [[] for _ in range(8)]
    for l in range(depth):
        lw = _layer_weights(l, g_mix_pre, g_mix_post, g_ffn_pre, g_ffn_post, w_in, g_cq, g_ckv, w_uq, w_ukv,
                            mu_shift, w0, w2, a0, a2, g2, k_k, k_a, r_k, lnx_w, lnx_b, w_out, w_up, w_down, g8)
        q, k, v, ckv, kr, rw = _proj_call(h_p, lw, tab_p, PROMPT_TILE)
        o_att = _attn_call(q, k, v, bp, seq_pad)
        o_rw, st = _rwkv_prompt_call(rw, lw, bp, seq_pad, seq_real)
        h_p = _finish_call(h_p, o_att, o_rw, lw, PROMPT_TILE)
        outs[0].append(ckv.reshape(bp, seq_pad, KV_LORA)[:, :seq_real])
        outs[1].append(kr.reshape(bp, seq_pad, ROPE_DIM)[:, :seq_real])
        outs[2].append(jnp.swapaxes(st, -1, -2))
        outs[3].append(rw.reshape(bp, seq_pad, RW_COLS)[:, seq_real - 1])
        q, _, _, ckv, kr, rw = _proj_call(h_s, lw, tab_s, bd)
        q_lat = _qlat_call(q, lw["w_uk_t"]).reshape(bd, H_A, KV_LORA)
        q_rope = q.reshape(bd, H_A, HEAD_PAD)[:, :, NOPE_DIM:QK_DIM].astype(F32)
        o_lat = _decode_attn_call(q_lat, q_rope, ckv.reshape(bd, 1, KV_LORA), kr.reshape(bd, 1, ROPE_DIM),
                                  cache_ckv, cache_krope_t, page_table, l)
        o_att = _ouv_call(o_lat.reshape(bd, H_A * KV_LORA), lw["w_uv_p"])
        o_rw, s_new = _rwkv_step_call(rw, state_shift[l], state_t, l, lw)
        h_s = _finish_call(h_s, o_att, o_rw, lw, bd)
        outs[4].append(ckv.reshape(bd, 1, KV_LORA))
        outs[5].append(kr.reshape(bd, 1, ROPE_DIM))
        outs[6].append(s_new)
        outs[7].append(rw)

    y_prompt = h_p.reshape(bp, seq_pad, D_MODEL)[:, N_META:seq_real]
    y_sample = h_s.reshape(bd, 1, D_MODEL)
    outs = [jnp.stack(o) for o in outs]
    outs[6] = jnp.transpose(outs[6], (0, 4, 1, 2, 3))
    return (y_prompt, y_sample) + tuple(outs)
```

```python
import functools
import math

import jax
import jax.numpy as jnp
from jax import lax
from jax.experimental import pallas as pl
from jax.experimental.pallas import tpu as pltpu

F32 = jnp.float32
BF16 = jnp.bfloat16

D_MODEL = 1024
N_META = 16
V_DIM = 64
NOPE_DIM = 64
ROPE_DIM = 32
QK_DIM = NOPE_DIM + ROPE_DIM
H_A = 8
Q_LORA = 384
KV_LORA = 256
ROPE_BASE = 10000.0
ATTN_SCALE = QK_DIM ** -0.5
N_R = 64
H_R = 8
RW_DIM = H_R * N_R
DECAY_LORA = 64
AAA_LORA = 64
GATE_LORA = 128
LNX_EPS = 64e-5
ATT_DIM = H_A * V_DIM
MLA_COLS = Q_LORA + KV_LORA + ROPE_DIM
RW_COLS = 3 * RW_DIM + DECAY_LORA + AAA_LORA + GATE_LORA
D_FF = 4 * D_MODEL
NORM_EPS = 1e-6
PAGE_SIZE = 128

LANES = 128
VMEM_LIMIT = 48 * 1024 * 1024
HEAD_PAD = LANES
QK_PAD = H_A * HEAD_PAD
IN_PAD = Q_LORA + KV_LORA + LANES + RW_COLS
ATT_BLK = 384
ATT_WIDE = 4
RW_CHUNK = 64
RW_STEP = 384
RW_GROUP = 3
PROMPT_TILE = 512

_NT = (((1,), (1,)), ((), ()))
_TN = (((0,), (0,)), ((), ()))
_NN = (((1,), (0,)), ((), ()))


def _dg(a, b, dims=_NN):
    return lax.dot_general(a, b, dims, preferred_element_type=F32)


def _dot(a, b, dims=_NN):
    return _dg(a.astype(BF16), b.astype(BF16), dims)


def _split(x):
    hi = x.astype(BF16)
    lo = (x - hi.astype(F32)).astype(BF16)
    return hi, lo


def _dot3(a, b, dims=_NN):
    ah, al = _split(a)
    bh, bl = _split(b)
    return _dg(ah, bh, dims) + (_dg(ah, bl, dims) + _dg(al, bh, dims))


def _rms(x, g):
    ms = jnp.mean(x * x, axis=-1, keepdims=True)
    return x * lax.rsqrt(ms + NORM_EPS) * g


def _const_spec(shape):
    nd = len(shape)
    return pl.BlockSpec(shape, lambda *_: (0,) * nd, pipeline_mode=pl.Buffered(1))


def _params(sem):
    return pltpu.CompilerParams(dimension_semantics=sem, vmem_limit_bytes=VMEM_LIMIT)


def _rope_chunk(x, tc, tm, tp):
    return x * tc + pltpu.roll(x, LANES - 16, 1) * tm + pltpu.roll(x, 16, 1) * tp


def _proj_kernel(h_ref, gpre_ref, win_ref, gcq_ref, gckv_ref, wuq_ref, wk_ref, wv_ref,
                 tc_ref, tm_ref, tp_ref, q_ref, k_ref, v_ref, ckv_ref, kr_ref, rw_ref):
    xn = _rms(h_ref[...], gpre_ref[...]).astype(BF16)
    tc, tm, tp = tc_ref[...], tm_ref[...], tp_ref[...]
    lane = lax.broadcasted_iota(jnp.int32, tc.shape, 1)

    cq = _rms(_dg(xn, win_ref[:, 0:Q_LORA]), gcq_ref[...])
    qf = _dot(cq, wuq_ref[...])
    for h in range(H_A):
        x = qf[:, h * HEAD_PAD:(h + 1) * HEAD_PAD]
        y = jnp.where(lane < NOPE_DIM, x, _rope_chunk(x, tc, tm, tp)) * ATTN_SCALE
        q_ref[:, h * HEAD_PAD:(h + 1) * HEAD_PAD] = y.astype(BF16)

    ckv = _rms(_dg(xn, win_ref[:, Q_LORA:Q_LORA + KV_LORA]), gckv_ref[...])
    ckv_ref[...] = ckv
    cb = ckv.astype(BF16)
    v_ref[...] = _dg(cb, wv_ref[...]).astype(BF16)
    kf = _dg(cb, wk_ref[...])
    kr0 = Q_LORA + KV_LORA
    ykr = _rope_chunk(_dg(xn, win_ref[:, kr0:kr0 + LANES]), tc, tm, tp)
    kr_ref[...] = ykr[:, 0:ROPE_DIM]
    kadd = jnp.where(lane >= NOPE_DIM, ykr, 0.0)
    for h in range(H_A):
        k_ref[:, h * HEAD_PAD:(h + 1) * HEAD_PAD] = (kf[:, h * HEAD_PAD:(h + 1) * HEAD_PAD] + kadd).astype(BF16)

    rw_ref[...] = _dg(xn, win_ref[:, kr0 + LANES:IN_PAD])


def _proj_call(h, lw, tables, tile):
    rows = h.shape[0]
    row = lambda n: pl.BlockSpec((tile, n), lambda i: (i, 0))
    return pl.pallas_call(
        _proj_kernel,
        grid=(rows // tile,),
        in_specs=[row(D_MODEL), _const_spec((1, D_MODEL)), _const_spec((D_MODEL, IN_PAD)),
                  _const_spec((1, Q_LORA)), _const_spec((1, KV_LORA)), _const_spec((Q_LORA, QK_PAD)),
                  _const_spec((KV_LORA, QK_PAD)), _const_spec((KV_LORA, ATT_DIM)),
                  row(LANES), row(LANES), row(LANES)],
        out_specs=[row(QK_PAD), row(QK_PAD), row(ATT_DIM), row(KV_LORA), row(ROPE_DIM), row(RW_COLS)],
        out_shape=[jax.ShapeDtypeStruct((rows, QK_PAD), BF16), jax.ShapeDtypeStruct((rows, QK_PAD), BF16),
                   jax.ShapeDtypeStruct((rows, ATT_DIM), BF16), jax.ShapeDtypeStruct((rows, KV_LORA), F32),
                   jax.ShapeDtypeStruct((rows, ROPE_DIM), F32), jax.ShapeDtypeStruct((rows, RW_COLS), F32)],
        compiler_params=_params(("parallel",)),
    )(h, lw["g_pre"], lw["w_in"], lw["g_cq"], lw["g_ckv"], lw["w_uq"], lw["w_k"], lw["w_v"], *tables)


def _attn_kernel(q_ref, k_ref, v_ref, o_ref):
    qi = pl.program_id(2)
    q = q_ref[...]
    row = lax.broadcasted_iota(jnp.int32, (ATT_BLK, ATT_BLK), 0)
    col = lax.broadcasted_iota(jnp.int32, (ATT_BLK, ATT_BLK), 1)

    def step(blk0, nblk, carry, masked=False):
        start = pl.multiple_of(blk0 * ATT_BLK, ATT_BLK)
        ks = k_ref[pl.ds(start, nblk * ATT_BLK), :]
        vs = v_ref[pl.ds(start, nblk * ATT_BLK), :]
        out = []
        for hh in range(2):
            m, l, acc = carry[hh]
            s = _dg(q[:, hh * HEAD_PAD:(hh + 1) * HEAD_PAD], ks[:, hh * HEAD_PAD:(hh + 1) * HEAD_PAD], _NT)
            if masked:
                s = jnp.where(col <= row, s, -jnp.inf)
            m_new = jnp.maximum(m, jnp.max(s, axis=-1, keepdims=True))
            alpha = jnp.exp(m - m_new)
            p = jnp.exp(s - m_new)
            l = alpha * l + jnp.sum(p, axis=-1, keepdims=True)
            acc = alpha * acc + _dg(p.astype(BF16), vs)
            out.append((m_new, l, acc))
        return tuple(out)

    one = (jnp.full((ATT_BLK, 1), -jnp.inf, F32), jnp.zeros((ATT_BLK, 1), F32),
           jnp.zeros((ATT_BLK, 2 * V_DIM), F32))
    wide = ATT_WIDE
    carry = lax.fori_loop(0, qi // wide, lambda j, c: step(j * wide, wide, c), (one, one))
    rest = qi % wide
    done = qi - rest
    carry = lax.cond(rest >= 2, lambda c: step(done, 2, c), lambda c: c, carry)
    carry = lax.cond(rest % 2 == 1, lambda c: step(qi - 1, 1, c), lambda c: c, carry)
    (_, l0, a0), (_, l1, a1) = step(qi, 1, carry, masked=True)
    lane = lax.broadcasted_iota(jnp.int32, (ATT_BLK, 2 * V_DIM), 1)
    o_ref[...] = jnp.where(lane < V_DIM, a0 / l0, a1 / l1).astype(BF16)


def _attn_call(q, k, v, batch, seq_pad):
    nq = seq_pad // ATT_BLK
    return pl.pallas_call(
        _attn_kernel,
        grid=(batch, H_A // 2, nq),
        in_specs=[pl.BlockSpec((ATT_BLK, 2 * HEAD_PAD), lambda b, hp, qi: (b * nq + qi, hp)),
                  pl.BlockSpec((seq_pad, 2 * HEAD_PAD), lambda b, hp, qi: (b, hp)),
                  pl.BlockSpec((seq_pad, 2 * V_DIM), lambda b, hp, qi: (b, hp))],
        out_specs=pl.BlockSpec((ATT_BLK, 2 * V_DIM), lambda b, hp, qi: (b * nq + qi, hp)),
        out_shape=jax.ShapeDtypeStruct((batch * seq_pad, ATT_DIM), BF16),
        compiler_params=_params(("parallel", "parallel", "arbitrary")),
    )(q, k, v)


def _rwkv_pre(x, prev, mu, w0, w2p, a0, a2p, g2, k_k, k_a, g8):
    xs = x + (prev - x) * mu
    r = xs[:, 0:RW_DIM]
    k = xs[:, RW_DIM:2 * RW_DIM]
    v = xs[:, 2 * RW_DIM:3 * RW_DIM]
    wa = xs[:, 3 * RW_DIM:3 * RW_DIM + LANES]
    gd = xs[:, 3 * RW_DIM + LANES:RW_COLS]
    z = w0 + _dot3(jnp.tanh(wa), w2p)
    lw = -math.exp(-0.5) * jax.nn.sigmoid(z)
    a = jax.nn.sigmoid(a0 + _dot(wa, a2p))
    g = _dot(jax.nn.sigmoid(gd), g2)
    kkr = k * k_k
    kk = kkr * lax.rsqrt(jnp.maximum(_dot(kkr * kkr, g8), 1e-24))
    k2 = k * (1.0 + (a - 1.0) * k_a)
    return r, k2, v, kk, a, lw, g


def _rwkv_post(y, r, k2, v, g, r_k, lnx_w, lnx_b, g8):
    mean = _dot(y, g8) * (1.0 / N_R)
    d = y - mean
    var = _dot(d * d, g8) * (1.0 / N_R)
    yn = d * lax.rsqrt(var + LNX_EPS) * lnx_w + lnx_b
    bonus = _dot(r * k2 * r_k, g8) * v
    return (yn + bonus) * g


def _rwkv_prompt_kernel(seq_real, rw_ref, mu_ref, w0_ref, w2_ref, a0_ref, a2_ref, g2_ref, kk_ref, ka_ref,
                        rk_ref, lnw_ref, lnb_ref, g8_ref, o_ref, st_ref,
                        st_scr, last_scr, rt_scr, at_scr, bt_scr, kt_scr, bw_scr, kw_scr, v_scr, wc_scr, y_scr):
    t = pl.program_id(1)
    C = RW_CHUNK

    @pl.when(t == 0)
    def _():
        st_scr[...] = jnp.zeros_like(st_scr)
        last_scr[...] = jnp.zeros_like(last_scr)

    x = rw_ref[...]
    rows = lax.broadcasted_iota(jnp.int32, (RW_STEP, 1), 0)
    prev = jnp.where(rows == 0, last_scr[...], pltpu.roll(x, 1, 0))
    last_scr[...] = x[RW_STEP - 1:RW_STEP, :]
    g8 = g8_ref[...]
    r, k2, v, kk, a, lw, g = _rwkv_pre(x, prev, mu_ref[...], w0_ref[...], w2_ref[...], a0_ref[...],
                                       a2_ref[...], g2_ref[...], kk_ref[...], ka_ref[...], g8)
    valid = (t * RW_STEP + rows) < seq_real
    lw = jnp.where(valid, lw, 0.0)
    kk = jnp.where(valid, kk, 0.0)
    k2m = jnp.where(valid, k2, 0.0)

    ri = lax.broadcasted_iota(jnp.int32, (LANES, LANES), 0)
    ci = lax.broadcasted_iota(jnp.int32, (LANES, LANES), 1)
    tri = jnp.where(((ri // C) == (ci // C)) & (ci <= ri), 1.0, 0.0).astype(BF16)
    lhi, llo = _split(lw)
    cum = jnp.concatenate([_dg(tri, lhi[i:i + LANES]) + _dg(tri, llo[i:i + LANES])
                           for i in range(0, RW_STEP, LANES)], 0)
    tot = jnp.concatenate([jnp.broadcast_to(cum[i + C - 1:i + C], (C, RW_DIM))
                           for i in range(0, RW_STEP, C)], 0)
    einv = jnp.exp(-cum)
    etail = jnp.exp(tot - cum)
    kb = kk * a
    rt_scr[...] = r * jnp.exp(cum)
    at_scr[...] = -kk * jnp.exp(cum - lw)
    bt_scr[...] = kb * einv
    kt_scr[...] = k2m * einv
    bw_scr[...] = kb * etail
    kw_scr[...] = k2m * etail
    v_scr[...] = v
    wc_scr[...] = jnp.exp(tot)

    r64 = lax.broadcasted_iota(jnp.int32, (C, C), 0)
    c64 = lax.broadcasted_iota(jnp.int32, (C, C), 1)
    strict = c64 < r64
    eye = c64 == r64
    r128 = lax.broadcasted_iota(jnp.int32, (C, 2 * C), 0)
    c128 = lax.broadcasted_iota(jnp.int32, (C, 2 * C), 1)
    incl2 = jnp.where(c128 >= C, c128 - C, c128) <= r128
    zeros64 = jnp.zeros((C, C), F32)

    def group(gi, carry):
        keys = [(cc, h) for cc in range(RW_GROUP) for h in range(H_R)]

        def rows(cc):
            return pl.ds(pl.multiple_of((gi * RW_GROUP + cc) * C, C), C)

        def ld(scr):
            return {(cc, h): scr[rows(cc), h * N_R:(h + 1) * N_R] for cc, h in keys}

        at, rt, v_ = ld(at_scr), ld(rt_scr), ld(v_scr)
        bt, kt = ld(bt_scr), ld(kt_scr)
        bk = {key: jnp.concatenate([bt[key], kt[key]], 0).astype(BF16) for key in keys}
        sc = {key: _dg(jnp.concatenate([at[key], rt[key]], 0).astype(BF16), bk[key], _NT) for key in keys}
        npow = {key: jnp.where(strict, sc[key][0:C, 0:C], 0.0).astype(BF16) for key in keys}
        a_ak = {key: jnp.where(strict, sc[key][0:C, C:2 * C], 0.0).astype(BF16) for key in keys}
        a_rbk = {key: jnp.where(incl2, sc[key][C:2 * C, :], 0.0).astype(BF16) for key in keys}
        vb = {key: v_[key].astype(BF16) for key in keys}
        xx = {key: jnp.concatenate([at[key], _dg(a_ak[key], vb[key])], 1) for key in keys}
        for j in range(6):
            xx = {key: xx[key] + _dg(npow[key], xx[key].astype(BF16)) for key in keys}
            if j < 5:
                npow = {key: _dg(npow[key], npow[key]).astype(BF16) for key in keys}
        zz = {key: jnp.concatenate([xx[key], jnp.concatenate([zeros64, v_[key]], 1)], 0).astype(BF16)
              for key in keys}
        yz = {key: _dg(a_rbk[key], zz[key]) for key in keys}
        bw, kw = ld(bw_scr), ld(kw_scr)
        bkw = {key: jnp.concatenate([bw[key], kw[key]], 0).astype(BF16) for key in keys}
        mz = {key: _dg(bkw[key], zz[key], _TN) for key in keys}
        wc = ld(wc_scr)
        st = [st_scr[h] for h in range(H_R)]
        for cc in range(RW_GROUP):
            stb = [s.astype(BF16) for s in st]
            for h in range(H_R):
                key = (cc, h)
                y_scr[rows(cc), h * N_R:(h + 1) * N_R] = (
                    _dg((rt[key] + yz[key][:, 0:C]).astype(BF16), stb[h]) + yz[key][:, C:2 * C])
                m = jnp.where(eye, wc[key], 0.0) + mz[key][:, 0:C]
                st[h] = _dg(m.astype(BF16), stb[h]) + mz[key][:, C:2 * C]
        for h in range(H_R):
            st_scr[h] = st[h]
        return carry

    lax.fori_loop(0, RW_STEP // C // RW_GROUP, group, 0)

    out = _rwkv_post(y_scr[...], r, k2, v, g, rk_ref[...], lnw_ref[...], lnb_ref[...], g8)
    o_ref[...] = out.astype(BF16)

    @pl.when(t == pl.num_programs(1) - 1)
    def _():
        st_ref[0] = st_scr[...]


def _rwkv_prompt_call(rw, lw, batch, seq_pad, seq_real):
    nt = seq_pad // RW_STEP
    vec = lambda n: _const_spec((1, n))
    big = pltpu.VMEM((RW_STEP, RW_DIM), F32)
    return pl.pallas_call(
        functools.partial(_rwkv_prompt_kernel, seq_real),
        grid=(batch, nt),
        in_specs=[pl.BlockSpec((RW_STEP, RW_COLS), lambda b, t: (b * nt + t, 0)),
                  vec(RW_COLS), vec(RW_DIM), _const_spec((LANES, RW_DIM)), vec(RW_DIM),
                  _const_spec((LANES, RW_DIM)), _const_spec((GATE_LORA, RW_DIM)), vec(RW_DIM), vec(RW_DIM),
                  vec(RW_DIM), vec(RW_DIM), vec(RW_DIM), _const_spec((RW_DIM, RW_DIM))],
        out_specs=[pl.BlockSpec((RW_STEP, RW_DIM), lambda b, t: (b * nt + t, 0)),
                   pl.BlockSpec((1, H_R, N_R, N_R), lambda b, t: (b, 0, 0, 0))],
        out_shape=[jax.ShapeDtypeStruct((batch * seq_pad, RW_DIM), BF16),
                   jax.ShapeDtypeStruct((batch, H_R, N_R, N_R), F32)],
        scratch_shapes=[pltpu.VMEM((H_R, N_R, N_R), F32), pltpu.VMEM((1, RW_COLS), F32),
                        big, big, big, big, big, big, big, big, big],
        compiler_params=_params(("arbitrary", "arbitrary")),
    )(rw, lw["mu"], lw["w0"], lw["w2"], lw["a0"], lw["a2"], lw["g2"], lw["k_k"], lw["k_a"],
      lw["r_k"], lw["lnx_w"], lw["lnx_b"], lw["g8"])


def _rwkv_step_kernel(rw_ref, sh_ref, s_ref, mu_ref, w0_ref, w2_ref, a0_ref, a2_ref, g2_ref, kk_ref, ka_ref,
                      rk_ref, lnw_ref, lnb_ref, g8_ref, o_ref, so_ref,
                      r_scr, k_scr, v_scr, g_scr, at_scr, bt_scr, wt_scr, kt_scr, vt_scr, rt_scr, yt_scr):
    h = pl.program_id(0)

    @pl.when(h == 0)
    def _():
        r, k2, v, kk, a, lw, g = _rwkv_pre(rw_ref[...], sh_ref[...], mu_ref[...], w0_ref[...], w2_ref[...],
                                           a0_ref[...], a2_ref[...], g2_ref[...], kk_ref[...], ka_ref[...],
                                           g8_ref[...])
        r_scr[...], k_scr[...], v_scr[...], g_scr[...] = r, k2, v, g
        at_scr[...] = (-kk).T
        bt_scr[...] = (kk * a).T
        wt_scr[...] = jnp.exp(lw).T
        kt_scr[...] = k2.T
        vt_scr[...] = v.T
        rt_scr[...] = r.T

    hrows = pl.ds(pl.multiple_of(h * N_R, N_R), N_R)
    a_h, b_h, w_h, k_h, r_h = at_scr[hrows, :], bt_scr[hrows, :], wt_scr[hrows, :], kt_scr[hrows, :], rt_scr[hrows, :]

    def vblock(vb, carry):
        v0 = pl.multiple_of(vb * 8, 8)
        vv = vt_scr[pl.ds(h * N_R + v0, 8), :]
        ys = []
        for i in range(8):
            s = s_ref[v0 + i]
            sa = jnp.sum(s * a_h, axis=0, keepdims=True)
            s2 = s * w_h + sa * b_h + vv[i:i + 1, :] * k_h
            so_ref[v0 + i] = s2
            ys.append(jnp.sum(s2 * r_h, axis=0, keepdims=True))
        yt_scr[pl.ds(h * N_R + v0, 8), :] = jnp.concatenate(ys, 0)
        return carry

    lax.fori_loop(0, N_R // 8, vblock, 0)

    @pl.when(h == pl.num_programs(0) - 1)
    def _():
        o_ref[...] = _rwkv_post(yt_scr[...].T, r_scr[...], k_scr[...], v_scr[...], g_scr[...], rk_ref[...],
                                lnw_ref[...], lnb_ref[...], g8_ref[...])


def _rwkv_step_call(rw, shift, state_t, layer, lw):
    nb = rw.shape[0]
    vec = lambda n: _const_spec((1, n))
    rows = pltpu.VMEM((nb, RW_DIM), F32)
    cols = pltpu.VMEM((RW_DIM, nb), F32)
    return pl.pallas_call(
        _rwkv_step_kernel,
        grid=(H_R,),
        in_specs=[_const_spec((nb, RW_COLS)), _const_spec((nb, RW_COLS)),
                  pl.BlockSpec((None, None, N_R, N_R, nb), lambda h: (layer, h, 0, 0, 0)),
                  vec(RW_COLS), vec(RW_DIM), _const_spec((LANES, RW_DIM)), vec(RW_DIM),
                  _const_spec((LANES, RW_DIM)), _const_spec((GATE_LORA, RW_DIM)), vec(RW_DIM), vec(RW_DIM),
                  vec(RW_DIM), vec(RW_DIM), vec(RW_DIM), _const_spec((RW_DIM, RW_DIM))],
        out_specs=[pl.BlockSpec((nb, RW_DIM), lambda h: (0, 0)),
                   pl.BlockSpec((None, N_R, N_R, nb), lambda h: (h, 0, 0, 0))],
        out_shape=[jax.ShapeDtypeStruct((nb, RW_DIM), F32),
                   jax.ShapeDtypeStruct((H_R, N_R, N_R, nb), F32)],
        scratch_shapes=[rows, rows, rows, rows, cols, cols, cols, cols, cols, cols, cols],
        compiler_params=_params(("arbitrary",)),
    )(rw, shift, state_t, lw["mu"], lw["w0"], lw["w2"], lw["a0"], lw["a2"], lw["g2"], lw["k_k"],
      lw["k_a"], lw["r_k"], lw["lnx_w"], lw["lnx_b"], lw["g8"])


def _finish_kernel(h_ref, oa_ref, orw_ref, woa_ref, wor_ref, gpost_ref, gfpre_ref, gfpost_ref,
                   wup_ref, wdn_ref, out_ref):
    o = _dot(oa_ref[...], woa_ref[...]) + _dot(orw_ref[...], wor_ref[...])
    h1 = h_ref[...] + _rms(o, gpost_ref[...])
    xn = _rms(h1, gfpre_ref[...]).astype(BF16)
    acc = jnp.zeros(h1.shape, F32)
    for c in range(D_FF // D_MODEL):
        cs = slice(c * D_MODEL, (c + 1) * D_MODEL)
        u = jnp.square(jnp.maximum(_dg(xn, wup_ref[:, cs]), 0.0))
        acc = acc + _dg(u.astype(BF16), wdn_ref[cs, :])
    out_ref[...] = h1 + _rms(acc, gfpost_ref[...])


def _finish_call(h, o_att, o_rw, lw, tile):
    rows = h.shape[0]
    row = lambda n: pl.BlockSpec((tile, n), lambda i: (i, 0))
    vec = _const_spec((1, D_MODEL))
    return pl.pallas_call(
        _finish_kernel,
        grid=(rows // tile,),
        in_specs=[row(D_MODEL), row(ATT_DIM), row(RW_DIM), _const_spec((ATT_DIM, D_MODEL)),
                  _const_spec((RW_DIM, D_MODEL)), vec, vec, vec,
                  _const_spec((D_MODEL, D_FF)), _const_spec((D_FF, D_MODEL))],
        out_specs=row(D_MODEL),
        out_shape=jax.ShapeDtypeStruct((rows, D_MODEL), F32),
        compiler_params=_params(("parallel",)),
    )(h, o_att, o_rw, lw["w_out_att"], lw["w_out_rw"], lw["g_post"], lw["g_ffn_pre"], lw["g_ffn_post"],
      lw["w_up"], lw["w_down"])


def _qlat_kernel(q_ref, wuk_ref, o_ref):
    q = q_ref[...]
    for h in range(H_A):
        o_ref[:, h * KV_LORA:(h + 1) * KV_LORA] = _dg(q[:, h * HEAD_PAD:(h + 1) * HEAD_PAD], wuk_ref[h])


def _qlat_call(q, w_uk):
    nb = q.shape[0]
    return pl.pallas_call(
        _qlat_kernel,
        out_shape=jax.ShapeDtypeStruct((nb, H_A * KV_LORA), F32),
    )(q, w_uk)


def _ouv_kernel(ol_ref, wuv_ref, o_ref):
    ol = ol_ref[...]
    acc = jnp.zeros(o_ref.shape, F32)
    for h in range(H_A):
        acc = acc + _dot(ol[:, h * KV_LORA:(h + 1) * KV_LORA], wuv_ref[h])
    o_ref[...] = acc


def _ouv_call(o_lat, w_uv):
    nb = o_lat.shape[0]
    return pl.pallas_call(
        _ouv_kernel,
        out_shape=jax.ShapeDtypeStruct((nb, ATT_DIM), F32),
    )(o_lat, w_uv)


def _decode_attn_kernel(layer, pt_ref, ql_ref, qr_ref, cn_ref, kn_ref, ckv_hbm, kr_hbm, o_ref, cbuf, kbuf, sem):
    b = pl.program_id(0)
    n_pages = cbuf.shape[1]
    slot = b % 2

    def page_copies(seq, sl, i):
        pid = pt_ref[seq, i]
        return (pltpu.make_async_copy(ckv_hbm.at[layer, pid], cbuf.at[sl, i], sem.at[0, sl]),
                pltpu.make_async_copy(kr_hbm.at[layer, pid], kbuf.at[sl, i], sem.at[1, sl]))

    def start_all(seq, sl):
        def body(i, c):
            for cp in page_copies(seq, sl, i):
                cp.start()
            return c
        lax.fori_loop(0, n_pages, body, 0)

    @pl.when(b == 0)
    def _():
        start_all(0, 0)

    @pl.when(b + 1 < pl.num_programs(0))
    def _():
        start_all(b + 1, 1 - slot)

    def wait_body(i, c):
        for cp in page_copies(b, slot, i):
            cp.wait()
        return c
    lax.fori_loop(0, n_pages, wait_body, 0)

    ql = ql_ref[0]
    qr = qr_ref[0]
    qlb, qrb = ql.astype(BF16), qr.astype(BF16)
    scores, pages = [], []
    for i in range(n_pages):
        cb = cbuf[slot, i].astype(BF16)
        kb = kbuf[slot, i].astype(BF16)
        scores.append(_dg(qlb, cb, _NT) + _dg(qrb, kb))
        pages.append(cb)
    cn = cn_ref[0]
    kn = kn_ref[0]
    s_n = jnp.sum(ql * cn, axis=-1, keepdims=True) + jnp.sum(qr * kn, axis=-1, keepdims=True)
    s = jnp.concatenate(scores, 1)
    m = jnp.maximum(jnp.max(s, axis=-1, keepdims=True), s_n)
    p = jnp.exp(s - m)
    pn = jnp.exp(s_n - m)
    pb = p.astype(BF16)
    pv = pn * cn
    for i in range(n_pages):
        pv = pv + _dg(pb[:, i * PAGE_SIZE:(i + 1) * PAGE_SIZE], pages[i])
    o_ref[0] = pv / (jnp.sum(p, axis=-1, keepdims=True) + pn)


def _decode_attn_call(q_lat, q_rope, ckv_new, kr_new, cache_ckv, cache_krope_t, page_table, layer):
    nb, n_pages = page_table.shape
    per_b = lambda shape: pl.BlockSpec((1,) + shape, lambda b, pt: (b, 0, 0))
    hbm = pl.BlockSpec(memory_space=pl.ANY)
    grid_spec = pltpu.PrefetchScalarGridSpec(
        num_scalar_prefetch=1,
        grid=(nb,),
        in_specs=[per_b((H_A, KV_LORA)), per_b((H_A, ROPE_DIM)), per_b((1, KV_LORA)), per_b((1, ROPE_DIM)),
                  hbm, hbm],
        out_specs=per_b((H_A, KV_LORA)),
        scratch_shapes=[pltpu.VMEM((2, n_pages, PAGE_SIZE, KV_LORA), F32),
                        pltpu.VMEM((2, n_pages, ROPE_DIM, PAGE_SIZE), F32),
                        pltpu.SemaphoreType.DMA((2, 2))],
    )
    return pl.pallas_call(
        functools.partial(_decode_attn_kernel, layer),
        grid_spec=grid_spec,
        out_shape=jax.ShapeDtypeStruct((nb, H_A, KV_LORA), F32),
        compiler_params=_params(("arbitrary",)),
    )(page_table, q_lat, q_rope, ckv_new, kr_new, cache_ckv, cache_krope_t)


def _rope_tables(pos):
    half = ROPE_DIM // 2
    inv = ROPE_BASE ** (-jnp.arange(half, dtype=F32) / half)
    ang = pos.astype(F32)[:, None] * inv[None, :]
    cos, sin = jnp.cos(ang), jnp.sin(ang)
    z16, z32 = jnp.zeros_like(cos), jnp.zeros((pos.shape[0], ROPE_DIM), F32)
    seg_c = jnp.concatenate([cos, cos], 1)
    seg_m = jnp.concatenate([-sin, z16], 1)
    seg_p = jnp.concatenate([z16, sin], 1)
    lay = lambda seg: jnp.concatenate([seg, z32, seg, z32], 1)
    return lay(seg_c), lay(seg_m), lay(seg_p)


def _layer_weights(l, g_mix_pre, g_mix_post, g_ffn_pre, g_ffn_post, w_in, g_cq, g_ckv, w_uq, w_ukv, mu_shift,
                   w0, w2, a0, a2, g2, k_k, k_a, r_k, lnx_w, lnx_b, w_out, w_up, w_down, g8):
    row = lambda x: x[l].reshape(1, -1)
    wi = w_in[l]
    w_kr = wi[:, Q_LORA + KV_LORA:MLA_COLS]
    z = jnp.zeros_like(w_kr)
    w_in_p = jnp.concatenate([wi[:, :Q_LORA + KV_LORA], w_kr, z, w_kr, z, wi[:, MLA_COLS:]], 1).astype(BF16)
    w_uq_p = jnp.pad(w_uq[l].reshape(Q_LORA, H_A, QK_DIM), ((0, 0), (0, 0), (0, HEAD_PAD - QK_DIM)))
    wkv = w_ukv[l].reshape(KV_LORA, H_A, NOPE_DIM + V_DIM)
    w_uk, w_uv = wkv[..., :NOPE_DIM], wkv[..., NOPE_DIM:]
    w_k_p = jnp.pad(w_uk, ((0, 0), (0, 0), (0, HEAD_PAD - NOPE_DIM)))
    w_uk_t = jnp.pad(jnp.transpose(w_uk, (1, 2, 0)), ((0, 0), (0, HEAD_PAD - NOPE_DIM), (0, 0)))
    eye = jnp.eye(H_A, dtype=F32)
    w_uv_p = (jnp.transpose(w_uv, (1, 0, 2))[:, :, None, :] * eye[:, None, :, None]).reshape(H_A, KV_LORA, ATT_DIM)
    zl = jnp.zeros((DECAY_LORA, RW_DIM), F32)
    return {
        "g_pre": row(g_mix_pre), "g_post": row(g_mix_post), "g_ffn_pre": row(g_ffn_pre),
        "g_ffn_post": row(g_ffn_post), "g_cq": row(g_cq), "g_ckv": row(g_ckv),
        "w_in": w_in_p, "w_uq": w_uq_p.reshape(Q_LORA, QK_PAD).astype(BF16),
        "w_k": w_k_p.reshape(KV_LORA, QK_PAD).astype(BF16), "w_v": w_uv.reshape(KV_LORA, ATT_DIM).astype(BF16),
        "w_uk_t": w_uk_t.astype(BF16), "w_uv_p": w_uv_p.astype(BF16),
        "mu": row(mu_shift), "w0": row(w0), "a0": row(a0), "k_k": row(k_k), "k_a": row(k_a), "r_k": row(r_k),
        "lnx_w": row(lnx_w), "lnx_b": row(lnx_b),
        "w2": jnp.concatenate([w2[l], zl], 0), "a2": jnp.concatenate([zl, a2[l]], 0), "g2": g2[l], "g8": g8,
        "w_out_att": w_out[l, :ATT_DIM].astype(BF16), "w_out_rw": w_out[l, ATT_DIM:].astype(BF16),
        "w_up": w_up[l].astype(BF16), "w_down": w_down[l].astype(BF16),
    }


def kernel(x_prompt, x_sample, cache_ckv, cache_krope, state_wkv, state_shift, page_table, meta_tokens,
           g_mix_pre, g_mix_post, g_ffn_pre, g_ffn_post, w_in, g_cq, g_ckv, w_uq, w_ukv, mu_shift, w0, w2,
           a0, a2, g2, k_k, k_a, r_k, lnx_w, lnx_b, w_out, w_up, w_down):
    bp, seq, _ = x_prompt.shape
    bd, dec_seq, _ = x_sample.shape
    depth = w_in.shape[0]
    assert dec_seq == 1, "the decode kernels handle one new token per sequence"
    seq_real = seq + N_META
    seq_pad = -(-seq_real // ATT_BLK) * ATT_BLK
    assert seq_pad % RW_STEP == 0 and (bp * seq_pad) % PROMPT_TILE == 0
    past_len = page_table.shape[1] * cache_ckv.shape[2]

    meta = jnp.broadcast_to(meta_tokens[None].astype(x_prompt.dtype), (bp, N_META, D_MODEL))
    tail = jnp.zeros((bp, seq_pad - seq_real, D_MODEL), x_prompt.dtype)
    h_p = jnp.concatenate([meta, x_prompt, tail], 1).reshape(bp * seq_pad, D_MODEL)
    h_s = x_sample.reshape(bd, D_MODEL)
    tab_p = _rope_tables(jnp.tile(jnp.arange(seq_pad), bp))
    tab_s = _rope_tables(jnp.full((bd,), past_len, jnp.int32))
    hi = lax.broadcasted_iota(jnp.int32, (RW_DIM, RW_DIM), 0) // N_R
    hj = lax.broadcasted_iota(jnp.int32, (RW_DIM, RW_DIM), 1) // N_R
    g8 = (hi == hj).astype(BF16)
    cache_krope_t = jnp.swapaxes(cache_krope, 2, 3)
    state_t = jnp.transpose(state_wkv, (0, 2, 3, 4, 1))

    outs = [[] for _ in range(8)]
    for l in range(depth):
        lw = _layer_weights(l, g_mix_pre, g_mix_post, g_ffn_pre, g_ffn_post, w_in, g_cq, g_ckv, w_uq, w_ukv,
                            mu_shift, w0, w2, a0, a2, g2, k_k, k_a, r_k, lnx_w, lnx_b, w_out, w_up, w_down, g8)
        q, k, v, ckv, kr, rw = _proj_call(h_p, lw, tab_p, PROMPT_TILE)
        o_att = _attn_call(q, k, v, bp, seq_pad)
        o_rw, st = _rwkv_prompt_call(rw, lw, bp, seq_pad, seq_real)
        h_p = _finish_call(h_p, o_att, o_rw, lw, PROMPT_TILE)
        outs[0].append(ckv.reshape(bp, seq_pad, KV_LORA)[:, :seq_real])
        outs[1].append(kr.reshape(bp, seq_pad, ROPE_DIM)[:, :seq_real])
        outs[2].append(jnp.swapaxes(st, -1, -2))
        outs[3].append(rw.reshape(bp, seq_pad, RW_COLS)[:, seq_real - 1])
        q, _, _, ckv, kr, rw = _proj_call(h_s, lw, tab_s, bd)
        q_lat = _qlat_call(q, lw["w_uk_t"]).reshape(bd, H_A, KV_LORA)
        q_rope = q.reshape(bd, H_A, HEAD_PAD)[:, :, NOPE_DIM:QK_DIM].astype(F32)
        o_lat = _decode_attn_call(q_lat, q_rope, ckv.reshape(bd, 1, KV_LORA), kr.reshape(bd, 1, ROPE_DIM),
                                  cache_ckv, cache_krope_t, page_table, l)
        o_att = _ouv_call(o_lat.reshape(bd, H_A * KV_LORA), lw["w_uv_p"])
        o_rw, s_new = _rwkv_step_call(rw, state_shift[l], state_t, l, lw)
        h_s = _finish_call(h_s, o_att, o_rw, lw, bd)
        outs[4].append(ckv.reshape(bd, 1, KV_LORA))
        outs[5].append(kr.reshape(bd, 1, ROPE_DIM))
        outs[6].append(s_new)
        outs[7].append(rw)

    y_prompt = h_p.reshape(bp, seq_pad, D_MODEL)[:, N_META:seq_real]
    y_sample = h_s.reshape(bd, 1, D_MODEL)
    outs = [jnp.stack(o) for o in outs]
    outs[6] = jnp.transpose(outs[6], (0, 4, 1, 2, 3))
    return (y_prompt, y_sample) + tuple(outs)
```

```python
import functools
import math

import jax
import jax.numpy as jnp
from jax import lax
from jax.experimental import pallas as pl
from jax.experimental.pallas import tpu as pltpu

F32 = jnp.float32
BF16 = jnp.bfloat16

D_MODEL = 1024
N_META = 16
V_DIM = 64
NOPE_DIM = 64
ROPE_DIM = 32
QK_DIM = NOPE_DIM + ROPE_DIM
H_A = 8
Q_LORA = 384
KV_LORA = 256
ROPE_BASE = 10000.0
ATTN_SCALE = QK_DIM ** -0.5
N_R = 64
H_R = 8
RW_DIM = H_R * N_R
DECAY_LORA = 64
AAA_LORA = 64
GATE_LORA = 128
LNX_EPS = 64e-5
ATT_DIM = H_A * V_DIM
MLA_COLS = Q_LORA + KV_LORA + ROPE_DIM
RW_COLS = 3 * RW_DIM + DECAY_LORA + AAA_LORA + GATE_LORA
D_FF = 4 * D_MODEL
NORM_EPS = 1e-6
PAGE_SIZE = 128

LANES = 128
VMEM_LIMIT = 48 * 1024 * 1024
HEAD_PAD = LANES
QK_PAD = H_A * HEAD_PAD
IN_PAD = Q_LORA + KV_LORA + LANES + RW_COLS
ATT_BLK = 384
ATT_WIDE = 4
RW_CHUNK = 64
RW_STEP = 384
RW_GROUP = 6
PROMPT_TILE = 512

_NT = (((1,), (1,)), ((), ()))
_TN = (((0,), (0,)), ((), ()))
_NN = (((1,), (0,)), ((), ()))


def _dg(a, b, dims=_NN):
    return lax.dot_general(a, b, dims, preferred_element_type=F32)


def _dot(a, b, dims=_NN):
    return _dg(a.astype(BF16), b.astype(BF16), dims)


def _split(x):
    hi = x.astype(BF16)
    lo = (x - hi.astype(F32)).astype(BF16)
    return hi, lo


def _dot3(a, b, dims=_NN):
    ah, al = _split(a)
    bh, bl = _split(b)
    return _dg(ah, bh, dims) + (_dg(ah, bl, dims) + _dg(al, bh, dims))


def _rms(x, g):
    ms = jnp.mean(x * x, axis=-1, keepdims=True)
    return x * lax.rsqrt(ms + NORM_EPS) * g


def _const_spec(shape):
    nd = len(shape)
    return pl.BlockSpec(shape, lambda *_: (0,) * nd, pipeline_mode=pl.Buffered(1))


def _params(sem):
    return pltpu.CompilerParams(dimension_semantics=sem, vmem_limit_bytes=VMEM_LIMIT)


def _rope_chunk(x, tc, tm, tp):
    return x * tc + pltpu.roll(x, LANES - 16, 1) * tm + pltpu.roll(x, 16, 1) * tp


def _proj_kernel(h_ref, gpre_ref, win_ref, gcq_ref, gckv_ref, wuq_ref, wk_ref, wv_ref,
                 tc_ref, tm_ref, tp_ref, q_ref, k_ref, v_ref, ckv_ref, kr_ref, rw_ref):
    xn = _rms(h_ref[...], gpre_ref[...]).astype(BF16)
    tc, tm, tp = tc_ref[...], tm_ref[...], tp_ref[...]
    lane = lax.broadcasted_iota(jnp.int32, tc.shape, 1)

    cq = _rms(_dg(xn, win_ref[:, 0:Q_LORA]), gcq_ref[...])
    qf = _dot(cq, wuq_ref[...])
    for h in range(H_A):
        x = qf[:, h * HEAD_PAD:(h + 1) * HEAD_PAD]
        y = jnp.where(lane < NOPE_DIM, x, _rope_chunk(x, tc, tm, tp)) * ATTN_SCALE
        q_ref[:, h * HEAD_PAD:(h + 1) * HEAD_PAD] = y.astype(BF16)

    ckv = _rms(_dg(xn, win_ref[:, Q_LORA:Q_LORA + KV_LORA]), gckv_ref[...])
    ckv_ref[...] = ckv
    cb = ckv.astype(BF16)
    v_ref[...] = _dg(cb, wv_ref[...]).astype(BF16)
    kf = _dg(cb, wk_ref[...])
    kr0 = Q_LORA + KV_LORA
    ykr = _rope_chunk(_dg(xn, win_ref[:, kr0:kr0 + LANES]), tc, tm, tp)
    kr_ref[...] = ykr[:, 0:ROPE_DIM]
    kadd = jnp.where(lane >= NOPE_DIM, ykr, 0.0)
    for h in range(H_A):
        k_ref[:, h * HEAD_PAD:(h + 1) * HEAD_PAD] = (kf[:, h * HEAD_PAD:(h + 1) * HEAD_PAD] + kadd).astype(BF16)

    rw_ref[...] = _dg(xn, win_ref[:, kr0 + LANES:IN_PAD])


def _proj_call(h, lw, tables, tile):
    rows = h.shape[0]
    row = lambda n: pl.BlockSpec((tile, n), lambda i: (i, 0))
    return pl.pallas_call(
        _proj_kernel,
        grid=(rows // tile,),
        in_specs=[row(D_MODEL), _const_spec((1, D_MODEL)), _const_spec((D_MODEL, IN_PAD)),
                  _const_spec((1, Q_LORA)), _const_spec((1, KV_LORA)), _const_spec((Q_LORA, QK_PAD)),
                  _const_spec((KV_LORA, QK_PAD)), _const_spec((KV_LORA, ATT_DIM)),
                  row(LANES), row(LANES), row(LANES)],
        out_specs=[row(QK_PAD), row(QK_PAD), row(ATT_DIM), row(KV_LORA), row(ROPE_DIM), row(RW_COLS)],
        out_shape=[jax.ShapeDtypeStruct((rows, QK_PAD), BF16), jax.ShapeDtypeStruct((rows, QK_PAD), BF16),
                   jax.ShapeDtypeStruct((rows, ATT_DIM), BF16), jax.ShapeDtypeStruct((rows, KV_LORA), F32),
                   jax.ShapeDtypeStruct((rows, ROPE_DIM), F32), jax.ShapeDtypeStruct((rows, RW_COLS), F32)],
        compiler_params=_params(("parallel",)),
    )(h, lw["g_pre"], lw["w_in"], lw["g_cq"], lw["g_ckv"], lw["w_uq"], lw["w_k"], lw["w_v"], *tables)


def _attn_kernel(q_ref, k_ref, v_ref, o_ref):
    qi = pl.program_id(2)
    q = q_ref[...]

    def step(blk0, nblk, carry, masked=False):
        start = pl.multiple_of(blk0 * ATT_BLK, ATT_BLK)
        ks = k_ref[pl.ds(start, nblk * ATT_BLK), :]
        vs = v_ref[pl.ds(start, nblk * ATT_BLK), :]
        if masked:
            row = lax.broadcasted_iota(jnp.int32, (ATT_BLK, nblk * ATT_BLK), 0)
            col = lax.broadcasted_iota(jnp.int32, (ATT_BLK, nblk * ATT_BLK), 1)
            visible = col - (nblk - 1) * ATT_BLK <= row
        out = []
        for hh in range(2):
            m, l, acc = carry[hh]
            s = _dg(q[:, hh * HEAD_PAD:(hh + 1) * HEAD_PAD], ks[:, hh * HEAD_PAD:(hh + 1) * HEAD_PAD], _NT)
            if masked:
                s = jnp.where(visible, s, -jnp.inf)
            m_new = jnp.maximum(m, jnp.max(s, axis=-1, keepdims=True))
            alpha = jnp.exp(m - m_new)
            p = jnp.exp(s - m_new)
            l = alpha * l + jnp.sum(p, axis=-1, keepdims=True)
            acc = alpha * acc + _dg(p.astype(BF16), vs)
            out.append((m_new, l, acc))
        return tuple(out)

    one = (jnp.full((ATT_BLK, 1), -jnp.inf, F32), jnp.zeros((ATT_BLK, 1), F32),
           jnp.zeros((ATT_BLK, 2 * V_DIM), F32))
    wide = ATT_WIDE
    carry = lax.fori_loop(0, qi // wide, lambda j, c: step(j * wide, wide, c), (one, one))
    rest = qi % wide
    tails = [functools.partial(lambda n, c: step(qi - n, n + 1, c, masked=True), n) for n in range(wide)]
    (_, l0, a0), (_, l1, a1) = lax.switch(rest, tails, carry)
    lane = lax.broadcasted_iota(jnp.int32, (ATT_BLK, 2 * V_DIM), 1)
    o_ref[...] = jnp.where(lane < V_DIM, a0 / l0, a1 / l1).astype(BF16)


def _attn_call(q, k, v, batch, seq_pad):
    nq = seq_pad // ATT_BLK
    return pl.pallas_call(
        _attn_kernel,
        grid=(batch, H_A // 2, nq),
        in_specs=[pl.BlockSpec((ATT_BLK, 2 * HEAD_PAD), lambda b, hp, qi: (b * nq + qi, hp)),
                  pl.BlockSpec((seq_pad, 2 * HEAD_PAD), lambda b, hp, qi: (b, hp)),
                  pl.BlockSpec((seq_pad, 2 * V_DIM), lambda b, hp, qi: (b, hp))],
        out_specs=pl.BlockSpec((ATT_BLK, 2 * V_DIM), lambda b, hp, qi: (b * nq + qi, hp)),
        out_shape=jax.ShapeDtypeStruct((batch * seq_pad, ATT_DIM), BF16),
        compiler_params=_params(("parallel", "parallel", "arbitrary")),
    )(q, k, v)


def _rwkv_pre(x, prev, mu, w0, w2p, a0, a2p, g2, k_k, k_a, g8):
    xs = x + (prev - x) * mu
    r = xs[:, 0:RW_DIM]
    k = xs[:, RW_DIM:2 * RW_DIM]
    v = xs[:, 2 * RW_DIM:3 * RW_DIM]
    wa = xs[:, 3 * RW_DIM:3 * RW_DIM + LANES]
    gd = xs[:, 3 * RW_DIM + LANES:RW_COLS]
    z = w0 + _dot3(jnp.tanh(wa), w2p)
    lw = -math.exp(-0.5) * jax.nn.sigmoid(z)
    a = jax.nn.sigmoid(a0 + _dot(wa, a2p))
    g = _dot(jax.nn.sigmoid(gd), g2)
    kkr = k * k_k
    kk = kkr * lax.rsqrt(jnp.maximum(_dot(kkr * kkr, g8), 1e-24))
    k2 = k * (1.0 + (a - 1.0) * k_a)
    return r, k2, v, kk, a, lw, g


def _rwkv_post(y, r, k2, v, g, r_k, lnx_w, lnx_b, g8):
    mean = _dot(y, g8) * (1.0 / N_R)
    d = y - mean
    var = _dot(d * d, g8) * (1.0 / N_R)
    yn = d * lax.rsqrt(var + LNX_EPS) * lnx_w + lnx_b
    bonus = _dot(r * k2 * r_k, g8) * v
    return (yn + bonus) * g


def _rwkv_prompt_kernel(seq_real, rw_ref, mu_ref, w0_ref, w2_ref, a0_ref, a2_ref, g2_ref, kk_ref, ka_ref,
                        rk_ref, lnw_ref, lnb_ref, g8_ref, o_ref, st_ref,
                        st_scr, last_scr, rt_scr, at_scr, bt_scr, kt_scr, bw_scr, kw_scr, v_scr, wc_scr, y_scr):
    t = pl.program_id(1)
    C = RW_CHUNK

    @pl.when(t == 0)
    def _():
        st_scr[...] = jnp.zeros_like(st_scr)
        last_scr[...] = jnp.zeros_like(last_scr)

    x = rw_ref[...]
    rows = lax.broadcasted_iota(jnp.int32, (RW_STEP, 1), 0)
    prev = jnp.where(rows == 0, last_scr[...], pltpu.roll(x, 1, 0))
    last_scr[...] = x[RW_STEP - 1:RW_STEP, :]
    g8 = g8_ref[...]
    r, k2, v, kk, a, lw, g = _rwkv_pre(x, prev, mu_ref[...], w0_ref[...], w2_ref[...], a0_ref[...],
                                       a2_ref[...], g2_ref[...], kk_ref[...], ka_ref[...], g8)
    valid = (t * RW_STEP + rows) < seq_real
    lw = jnp.where(valid, lw, 0.0)
    kk = jnp.where(valid, kk, 0.0)
    k2m = jnp.where(valid, k2, 0.0)

    ri = lax.broadcasted_iota(jnp.int32, (LANES, LANES), 0)
    ci = lax.broadcasted_iota(jnp.int32, (LANES, LANES), 1)
    tri = jnp.where(((ri // C) == (ci // C)) & (ci <= ri), 1.0, 0.0).astype(BF16)
    lhi, llo = _split(lw)
    cum = jnp.concatenate([_dg(tri, lhi[i:i + LANES]) + _dg(tri, llo[i:i + LANES])
                           for i in range(0, RW_STEP, LANES)], 0)
    tot = jnp.concatenate([jnp.broadcast_to(cum[i + C - 1:i + C], (C, RW_DIM))
                           for i in range(0, RW_STEP, C)], 0)
    einv = jnp.exp(-cum)
    etail = jnp.exp(tot - cum)
    kb = kk * a
    rt_scr[...] = r * jnp.exp(cum)
    at_scr[...] = -kk * jnp.exp(cum - lw)
    bt_scr[...] = kb * einv
    kt_scr[...] = k2m * einv
    bw_scr[...] = kb * etail
    kw_scr[...] = k2m * etail
    v_scr[...] = v
    wc_scr[...] = jnp.exp(tot)

    r64 = lax.broadcasted_iota(jnp.int32, (C, C), 0)
    c64 = lax.broadcasted_iota(jnp.int32, (C, C), 1)
    strict = c64 < r64
    eye = c64 == r64
    r128 = lax.broadcasted_iota(jnp.int32, (C, 2 * C), 0)
    c128 = lax.broadcasted_iota(jnp.int32, (C, 2 * C), 1)
    incl2 = jnp.where(c128 >= C, c128 - C, c128) <= r128
    zeros64 = jnp.zeros((C, C), F32)

    def group(gi, carry):
        keys = [(cc, h) for cc in range(RW_GROUP) for h in range(H_R)]

        def rows(cc):
            return pl.ds(pl.multiple_of((gi * RW_GROUP + cc) * C, C), C)

        def ld(scr):
            return {(cc, h): scr[rows(cc), h * N_R:(h + 1) * N_R] for cc, h in keys}

        at, rt, v_ = ld(at_scr), ld(rt_scr), ld(v_scr)
        bt, kt = ld(bt_scr), ld(kt_scr)
        bk = {key: jnp.concatenate([bt[key], kt[key]], 0).astype(BF16) for key in keys}
        sc = {key: _dg(jnp.concatenate([at[key], rt[key]], 0).astype(BF16), bk[key], _NT) for key in keys}
        npow = {key: jnp.where(strict, sc[key][0:C, 0:C], 0.0).astype(BF16) for key in keys}
        a_ak = {key: jnp.where(strict, sc[key][0:C, C:2 * C], 0.0).astype(BF16) for key in keys}
        a_rbk = {key: jnp.where(incl2, sc[key][C:2 * C, :], 0.0).astype(BF16) for key in keys}
        vb = {key: v_[key].astype(BF16) for key in keys}
        xx = {key: jnp.concatenate([at[key], _dg(a_ak[key], vb[key])], 1) for key in keys}
        for j in range(6):
            xx = {key: xx[key] + _dg(npow[key], xx[key].astype(BF16)) for key in keys}
            if j < 5:
                npow = {key: _dg(npow[key], npow[key]).astype(BF16) for key in keys}
        zz = {key: jnp.concatenate([xx[key], jnp.concatenate([zeros64, v_[key]], 1)], 0).astype(BF16)
              for key in keys}
        yz = {key: _dg(a_rbk[key], zz[key]) for key in keys}
        bw, kw = ld(bw_scr), ld(kw_scr)
        bkw = {key: jnp.concatenate([bw[key], kw[key]], 0).astype(BF16) for key in keys}
        mz = {key: _dg(bkw[key], zz[key], _TN) for key in keys}
        wc = ld(wc_scr)
        st = [st_scr[h] for h in range(H_R)]
        for cc in range(RW_GROUP):
            stb = [s.astype(BF16) for s in st]
            for h in range(H_R):
                key = (cc, h)
                y_scr[rows(cc), h * N_R:(h + 1) * N_R] = (
                    _dg((rt[key] + yz[key][:, 0:C]).astype(BF16), stb[h]) + yz[key][:, C:2 * C])
                m = jnp.where(eye, wc[key], 0.0) + mz[key][:, 0:C]
                st[h] = _dg(m.astype(BF16), stb[h]) + mz[key][:, C:2 * C]
        for h in range(H_R):
            st_scr[h] = st[h]
        return carry

    lax.fori_loop(0, RW_STEP // C // RW_GROUP, group, 0)

    out = _rwkv_post(y_scr[...], r, k2, v, g, rk_ref[...], lnw_ref[...], lnb_ref[...], g8)
    o_ref[...] = out.astype(BF16)

    @pl.when(t == pl.num_programs(1) - 1)
    def _():
        st_ref[0] = st_scr[...]


def _rwkv_prompt_call(rw, lw, batch, seq_pad, seq_real):
    nt = seq_pad // RW_STEP
    vec = lambda n: _const_spec((1, n))
    big = pltpu.VMEM((RW_STEP, RW_DIM), F32)
    return pl.pallas_call(
        functools.partial(_rwkv_prompt_kernel, seq_real),
        grid=(batch, nt),
        in_specs=[pl.BlockSpec((RW_STEP, RW_COLS), lambda b, t: (b * nt + t, 0)),
                  vec(RW_COLS), vec(RW_DIM), _const_spec((LANES, RW_DIM)), vec(RW_DIM),
                  _const_spec((LANES, RW_DIM)), _const_spec((GATE_LORA, RW_DIM)), vec(RW_DIM), vec(RW_DIM),
                  vec(RW_DIM), vec(RW_DIM), vec(RW_DIM), _const_spec((RW_DIM, RW_DIM))],
        out_specs=[pl.BlockSpec((RW_STEP, RW_DIM), lambda b, t: (b * nt + t, 0)),
                   pl.BlockSpec((1, H_R, N_R, N_R), lambda b, t: (b, 0, 0, 0))],
        out_shape=[jax.ShapeDtypeStruct((batch * seq_pad, RW_DIM), BF16),
                   jax.ShapeDtypeStruct((batch, H_R, N_R, N_R), F32)],
        scratch_shapes=[pltpu.VMEM((H_R, N_R, N_R), F32), pltpu.VMEM((1, RW_COLS), F32),
                        big, big, big, big, big, big, big, big, big],
        compiler_params=_params(("arbitrary", "arbitrary")),
    )(rw, lw["mu"], lw["w0"], lw["w2"], lw["a0"], lw["a2"], lw["g2"], lw["k_k"], lw["k_a"],
      lw["r_k"], lw["lnx_w"], lw["lnx_b"], lw["g8"])


def _rwkv_step_kernel(rw_ref, sh_ref, s_ref, mu_ref, w0_ref, w2_ref, a0_ref, a2_ref, g2_ref, kk_ref, ka_ref,
                      rk_ref, lnw_ref, lnb_ref, g8_ref, o_ref, so_ref,
                      r_scr, k_scr, v_scr, g_scr, at_scr, bt_scr, wt_scr, kt_scr, vt_scr, rt_scr, yt_scr):
    h = pl.program_id(0)

    @pl.when(h == 0)
    def _():
        r, k2, v, kk, a, lw, g = _rwkv_pre(rw_ref[...], sh_ref[...], mu_ref[...], w0_ref[...], w2_ref[...],
                                           a0_ref[...], a2_ref[...], g2_ref[...], kk_ref[...], ka_ref[...],
                                           g8_ref[...])
        r_scr[...], k_scr[...], v_scr[...], g_scr[...] = r, k2, v, g
        at_scr[...] = (-kk).T
        bt_scr[...] = (kk * a).T
        wt_scr[...] = jnp.exp(lw).T
        kt_scr[...] = k2.T
        vt_scr[...] = v.T
        rt_scr[...] = r.T

    hrows = pl.ds(pl.multiple_of(h * N_R, N_R), N_R)
    a_h, b_h, w_h, k_h, r_h = at_scr[hrows, :], bt_scr[hrows, :], wt_scr[hrows, :], kt_scr[hrows, :], rt_scr[hrows, :]

    def vblock(vb, carry):
        v0 = pl.multiple_of(vb * 8, 8)
        vv = vt_scr[pl.ds(h * N_R + v0, 8), :]
        ys = []
        for i in range(8):
            s = s_ref[v0 + i]
            sa = jnp.sum(s * a_h, axis=0, keepdims=True)
            s2 = s * w_h + sa * b_h + vv[i:i + 1, :] * k_h
            so_ref[v0 + i] = s2
            ys.append(jnp.sum(s2 * r_h, axis=0, keepdims=True))
        yt_scr[pl.ds(h * N_R + v0, 8), :] = jnp.concatenate(ys, 0)
        return carry

    lax.fori_loop(0, N_R // 8, vblock, 0)

    @pl.when(h == pl.num_programs(0) - 1)
    def _():
        o_ref[...] = _rwkv_post(yt_scr[...].T, r_scr[...], k_scr[...], v_scr[...], g_scr[...], rk_ref[...],
                                lnw_ref[...], lnb_ref[...], g8_ref[...])


def _rwkv_step_call(rw, shift, state_t, layer, lw):
    nb = rw.shape[0]
    vec = lambda n: _const_spec((1, n))
    rows = pltpu.VMEM((nb, RW_DIM), F32)
    cols = pltpu.VMEM((RW_DIM, nb), F32)
    return pl.pallas_call(
        _rwkv_step_kernel,
        grid=(H_R,),
        in_specs=[_const_spec((nb, RW_COLS)), _const_spec((nb, RW_COLS)),
                  pl.BlockSpec((None, None, N_R, N_R, nb), lambda h: (layer, h, 0, 0, 0)),
                  vec(RW_COLS), vec(RW_DIM), _const_spec((LANES, RW_DIM)), vec(RW_DIM),
                  _const_spec((LANES, RW_DIM)), _const_spec((GATE_LORA, RW_DIM)), vec(RW_DIM), vec(RW_DIM),
                  vec(RW_DIM), vec(RW_DIM), vec(RW_DIM), _const_spec((RW_DIM, RW_DIM))],
        out_specs=[pl.BlockSpec((nb, RW_DIM), lambda h: (0, 0)),
                   pl.BlockSpec((None, N_R, N_R, nb), lambda h: (h, 0, 0, 0))],
        out_shape=[jax.ShapeDtypeStruct((nb, RW_DIM), F32),
                   jax.ShapeDtypeStruct((H_R, N_R, N_R, nb), F32)],
        scratch_shapes=[rows, rows, rows, rows, cols, cols, cols, cols, cols, cols, cols],
        compiler_params=_params(("arbitrary",)),
    )(rw, shift, state_t, lw["mu"], lw["w0"], lw["w2"], lw["a0"], lw["a2"], lw["g2"], lw["k_k"],
      lw["k_a"], lw["r_k"], lw["lnx_w"], lw["lnx_b"], lw["g8"])


def _finish_kernel(h_ref, oa_ref, orw_ref, woa_ref, wor_ref, gpost_ref, gfpre_ref, gfpost_ref,
                   wup_ref, wdn_ref, out_ref):
    o = _dot(oa_ref[...], woa_ref[...]) + _dot(orw_ref[...], wor_ref[...])
    h1 = h_ref[...] + _rms(o, gpost_ref[...])
    xn = _rms(h1, gfpre_ref[...]).astype(BF16)
    acc = jnp.zeros(h1.shape, F32)
    for c in range(D_FF // D_MODEL):
        cs = slice(c * D_MODEL, (c + 1) * D_MODEL)
        u = jnp.square(jnp.maximum(_dg(xn, wup_ref[:, cs]), 0.0))
        acc = acc + _dg(u.astype(BF16), wdn_ref[cs, :])
    out_ref[...] = h1 + _rms(acc, gfpost_ref[...])


def _finish_call(h, o_att, o_rw, lw, tile):
    rows = h.shape[0]
    row = lambda n: pl.BlockSpec((tile, n), lambda i: (i, 0))
    vec = _const_spec((1, D_MODEL))
    return pl.pallas_call(
        _finish_kernel,
        grid=(rows // tile,),
        in_specs=[row(D_MODEL), row(ATT_DIM), row(RW_DIM), _const_spec((ATT_DIM, D_MODEL)),
                  _const_spec((RW_DIM, D_MODEL)), vec, vec, vec,
                  _const_spec((D_MODEL, D_FF)), _const_spec((D_FF, D_MODEL))],
        out_specs=row(D_MODEL),
        out_shape=jax.ShapeDtypeStruct((rows, D_MODEL), F32),
        compiler_params=_params(("parallel",)),
    )(h, o_att, o_rw, lw["w_out_att"], lw["w_out_rw"], lw["g_post"], lw["g_ffn_pre"], lw["g_ffn_post"],
      lw["w_up"], lw["w_down"])


def _qlat_kernel(q_ref, wuk_ref, o_ref):
    q = q_ref[...]
    for h in range(H_A):
        o_ref[:, h * KV_LORA:(h + 1) * KV_LORA] = _dg(q[:, h * HEAD_PAD:(h + 1) * HEAD_PAD], wuk_ref[h])


def _qlat_call(q, w_uk):
    nb = q.shape[0]
    return pl.pallas_call(
        _qlat_kernel,
        out_shape=jax.ShapeDtypeStruct((nb, H_A * KV_LORA), F32),
    )(q, w_uk)


def _ouv_kernel(ol_ref, wuv_ref, o_ref):
    ol = ol_ref[...]
    acc = jnp.zeros(o_ref.shape, F32)
    for h in range(H_A):
        acc = acc + _dot(ol[:, h * KV_LORA:(h + 1) * KV_LORA], wuv_ref[h])
    o_ref[...] = acc


def _ouv_call(o_lat, w_uv):
    nb = o_lat.shape[0]
    return pl.pallas_call(
        _ouv_kernel,
        out_shape=jax.ShapeDtypeStruct((nb, ATT_DIM), F32),
    )(o_lat, w_uv)


def _decode_attn_kernel(layer, pt_ref, ql_ref, qr_ref, cn_ref, kn_ref, ckv_hbm, kr_hbm, o_ref, cbuf, kbuf, sem):
    b = pl.program_id(0)
    n_pages = cbuf.shape[1]
    slot = b % 2

    def page_copies(seq, sl, i):
        pid = pt_ref[seq, i]
        return (pltpu.make_async_copy(ckv_hbm.at[layer, pid], cbuf.at[sl, i], sem.at[0, sl]),
                pltpu.make_async_copy(kr_hbm.at[layer, pid],
                                      kbuf.at[sl, :, pl.ds(pl.multiple_of(i * PAGE_SIZE, PAGE_SIZE), PAGE_SIZE)],
                                      sem.at[1, sl]))

    def wait_all(seq, sl):
        def body(i, c):
            for cp in page_copies(seq, sl, i):
                cp.wait()
            return c
        lax.fori_loop(0, n_pages, body, 0)

    @pl.when(b == 0)
    def _():
        def body(i, c):
            for cp in page_copies(0, 0, i):
                cp.start()
            return c
        lax.fori_loop(0, n_pages, body, 0)

    wait_all(b, slot)
    nxt = jnp.minimum(b + 1, pl.num_programs(0) - 1)
    for i in range(n_pages):
        for cp in page_copies(nxt, 1 - slot, i):
            cp.start()

    ql = ql_ref[0]
    qr = qr_ref[0]
    cb = cbuf[slot].reshape(n_pages * PAGE_SIZE, KV_LORA).astype(BF16)
    s = _dg(ql.astype(BF16), cb, _NT) + _dg(qr.astype(BF16), kbuf[slot].astype(BF16))
    cn = cn_ref[0]
    kn = kn_ref[0]
    s_n = jnp.sum(ql * cn, axis=-1, keepdims=True) + jnp.sum(qr * kn, axis=-1, keepdims=True)
    m = jnp.maximum(jnp.max(s, axis=-1, keepdims=True), s_n)
    p = jnp.exp(s - m)
    pn = jnp.exp(s_n - m)
    pv = _dg(p.astype(BF16), cb) + pn * cn
    o_ref[0] = pv / (jnp.sum(p, axis=-1, keepdims=True) + pn)

    @pl.when(b == pl.num_programs(0) - 1)
    def _():
        wait_all(nxt, 1 - slot)


def _decode_attn_call(q_lat, q_rope, ckv_new, kr_new, cache_ckv, cache_krope_t, page_table, layer):
    nb, n_pages = page_table.shape
    per_b = lambda shape: pl.BlockSpec((1,) + shape, lambda b, pt: (b, 0, 0))
    hbm = pl.BlockSpec(memory_space=pl.ANY)
    grid_spec = pltpu.PrefetchScalarGridSpec(
        num_scalar_prefetch=1,
        grid=(nb,),
        in_specs=[per_b((H_A, KV_LORA)), per_b((H_A, ROPE_DIM)), per_b((1, KV_LORA)), per_b((1, ROPE_DIM)),
                  hbm, hbm],
        out_specs=per_b((H_A, KV_LORA)),
        scratch_shapes=[pltpu.VMEM((2, n_pages, PAGE_SIZE, KV_LORA), F32),
                        pltpu.VMEM((2, ROPE_DIM, n_pages * PAGE_SIZE), F32),
                        pltpu.SemaphoreType.DMA((2, 2))],
    )
    return pl.pallas_call(
        functools.partial(_decode_attn_kernel, layer),
        grid_spec=grid_spec,
        out_shape=jax.ShapeDtypeStruct((nb, H_A, KV_LORA), F32),
        compiler_params=_params(("arbitrary",)),
    )(page_table, q_lat, q_rope, ckv_new, kr_new, cache_ckv, cache_krope_t)


def _rope_tables(pos):
    half = ROPE_DIM // 2
    inv = ROPE_BASE ** (-jnp.arange(half, dtype=F32) / half)
    ang = pos.astype(F32)[:, None] * inv[None, :]
    cos, sin = jnp.cos(ang), jnp.sin(ang)
    z16, z32 = jnp.zeros_like(cos), jnp.zeros((pos.shape[0], ROPE_DIM), F32)
    seg_c = jnp.concatenate([cos, cos], 1)
    seg_m = jnp.concatenate([-sin, z16], 1)
    seg_p = jnp.concatenate([z16, sin], 1)
    lay = lambda seg: jnp.concatenate([seg, z32, seg, z32], 1)
    return lay(seg_c), lay(seg_m), lay(seg_p)


def _layer_weights(l, g_mix_pre, g_mix_post, g_ffn_pre, g_ffn_post, w_in, g_cq, g_ckv, w_uq, w_ukv, mu_shift,
                   w0, w2, a0, a2, g2, k_k, k_a, r_k, lnx_w, lnx_b, w_out, w_up, w_down, g8):
    row = lambda x: x[l].reshape(1, -1)
    wi = w_in[l]
    w_kr = wi[:, Q_LORA + KV_LORA:MLA_COLS]
    z = jnp.zeros_like(w_kr)
    w_in_p = jnp.concatenate([wi[:, :Q_LORA + KV_LORA], w_kr, z, w_kr, z, wi[:, MLA_COLS:]], 1).astype(BF16)
    w_uq_p = jnp.pad(w_uq[l].reshape(Q_LORA, H_A, QK_DIM), ((0, 0), (0, 0), (0, HEAD_PAD - QK_DIM)))
    wkv = w_ukv[l].reshape(KV_LORA, H_A, NOPE_DIM + V_DIM)
    w_uk, w_uv = wkv[..., :NOPE_DIM], wkv[..., NOPE_DIM:]
    w_k_p = jnp.pad(w_uk, ((0, 0), (0, 0), (0, HEAD_PAD - NOPE_DIM)))
    w_uk_t = jnp.pad(jnp.transpose(w_uk, (1, 2, 0)), ((0, 0), (0, HEAD_PAD - NOPE_DIM), (0, 0)))
    eye = jnp.eye(H_A, dtype=F32)
    w_uv_p = (jnp.transpose(w_uv, (1, 0, 2))[:, :, None, :] * eye[:, None, :, None]).reshape(H_A, KV_LORA, ATT_DIM)
    zl = jnp.zeros((DECAY_LORA, RW_DIM), F32)
    return {
        "g_pre": row(g_mix_pre), "g_post": row(g_mix_post), "g_ffn_pre": row(g_ffn_pre),
        "g_ffn_post": row(g_ffn_post), "g_cq": row(g_cq), "g_ckv": row(g_ckv),
        "w_in": w_in_p, "w_uq": w_uq_p.reshape(Q_LORA, QK_PAD).astype(BF16),
        "w_k": w_k_p.reshape(KV_LORA, QK_PAD).astype(BF16), "w_v": w_uv.reshape(KV_LORA, ATT_DIM).astype(BF16),
        "w_uk_t": w_uk_t.astype(BF16), "w_uv_p": w_uv_p.astype(BF16),
        "mu": row(mu_shift), "w0": row(w0), "a0": row(a0), "k_k": row(k_k), "k_a": row(k_a), "r_k": row(r_k),
        "lnx_w": row(lnx_w), "lnx_b": row(lnx_b),
        "w2": jnp.concatenate([w2[l], zl], 0), "a2": jnp.concatenate([zl, a2[l]], 0), "g2": g2[l], "g8": g8,
        "w_out_att": w_out[l, :ATT_DIM].astype(BF16), "w_out_rw": w_out[l, ATT_DIM:].astype(BF16),
        "w_up": w_up[l].astype(BF16), "w_down": w_down[l].astype(BF16),
    }


def kernel(x_prompt, x_sample, cache_ckv, cache_krope, state_wkv, state_shift, page_table, meta_tokens,
           g_mix_pre, g_mix_post, g_ffn_pre, g_ffn_post, w_in, g_cq, g_ckv, w_uq, w_ukv, mu_shift, w0, w2,
           a0, a2, g2, k_k, k_a, r_k, lnx_w, lnx_b, w_out, w_up, w_down):
    bp, seq, _ = x_prompt.shape
    bd, dec_seq, _ = x_sample.shape
    depth = w_in.shape[0]
    assert dec_seq == 1, "the decode kernels handle one new token per sequence"
    seq_real = seq + N_META
    seq_pad = -(-seq_real // ATT_BLK) * ATT_BLK
    assert seq_pad % RW_STEP == 0 and (bp * seq_pad) % PROMPT_TILE == 0
    past_len = page_table.shape[1] * cache_ckv.shape[2]

    meta = jnp.broadcast_to(meta_tokens[None].astype(x_prompt.dtype), (bp, N_META, D_MODEL))
    h_p = jnp.pad(x_prompt, ((0, 0), (N_META, seq_pad - seq_real), (0, 0)))
    h_p = lax.dynamic_update_slice(h_p, meta, (0, 0, 0)).reshape(bp * seq_pad, D_MODEL)
    h_s = x_sample.reshape(bd, D_MODEL)
    tab_p = _rope_tables(jnp.tile(jnp.arange(seq_pad), bp))
    tab_s = _rope_tables(jnp.full((bd,), past_len, jnp.int32))
    hi = lax.broadcasted_iota(jnp.int32, (RW_DIM, RW_DIM), 0) // N_R
    hj = lax.broadcasted_iota(jnp.int32, (RW_DIM, RW_DIM), 1) // N_R
    g8 = (hi == hj).astype(BF16)
    cache_krope_t = jnp.swapaxes(cache_krope, 2, 3)
    state_t = jnp.transpose(state_wkv, (0, 2, 3, 4, 1))

    outs = [[] for _ in range(8)]
    for l in range(depth):
        lw = _layer_weights(l, g_mix_pre, g_mix_post, g_ffn_pre, g_ffn_post, w_in, g_cq, g_ckv, w_uq, w_ukv,
                            mu_shift, w0, w2, a0, a2, g2, k_k, k_a, r_k, lnx_w, lnx_b, w_out, w_up, w_down, g8)
        q, k, v, ckv, kr, rw = _proj_call(h_p, lw, tab_p, PROMPT_TILE)
        o_att = _attn_call(q, k, v, bp, seq_pad)
        o_rw, st = _rwkv_prompt_call(rw, lw, bp, seq_pad, seq_real)
        h_p = _finish_call(h_p, o_att, o_rw, lw, PROMPT_TILE)
        outs[0].append(ckv.reshape(bp, seq_pad, KV_LORA)[:, :seq_real])
        outs[1].append(kr.reshape(bp, seq_pad, ROPE_DIM)[:, :seq_real])
        outs[2].append(jnp.swapaxes(st, -1, -2))
        outs[3].append(rw.reshape(bp, seq_pad, RW_COLS)[:, seq_real - 1])
        q, _, _, ckv, kr, rw = _proj_call(h_s, lw, tab_s, bd)
        q_lat = _qlat_call(q, lw["w_uk_t"]).reshape(bd, H_A, KV_LORA)
        q_rope = q.reshape(bd, H_A, HEAD_PAD)[:, :, NOPE_DIM:QK_DIM].astype(F32)
        o_lat = _decode_attn_call(q_lat, q_rope, ckv.reshape(bd, 1, KV_LORA), kr.reshape(bd, 1, ROPE_DIM),
                                  cache_ckv, cache_krope_t, page_table, l)
        o_att = _ouv_call(o_lat.reshape(bd, H_A * KV_LORA), lw["w_uv_p"])
        o_rw, s_new = _rwkv_step_call(rw, state_shift[l], state_t, l, lw)
        h_s = _finish_call(h_s, o_att, o_rw, lw, bd)
        outs[4].append(ckv.reshape(bd, 1, KV_LORA))
        outs[5].append(kr.reshape(bd, 1, ROPE_DIM))
        outs[6].append(s_new)
        outs[7].append(rw)

    y_prompt = h_p.reshape(bp, seq_pad, D_MODEL)[:, N_META:seq_real]
    y_sample = h_s.reshape(bd, 1, D_MODEL)
    outs = [jnp.stack(o) for o in outs]
    outs[6] = jnp.transpose(outs[6], (0, 4, 1, 2, 3))
    return (y_prompt, y_sample) + tuple(outs)
```

```python
import functools
import math

import jax
import jax.numpy as jnp
from jax import lax
from jax.experimental import pallas as pl
from jax.experimental.pallas import tpu as pltpu

F32 = jnp.float32
BF16 = jnp.bfloat16

D_MODEL = 1024
N_META = 16
V_DIM = 64
NOPE_DIM = 64
ROPE_DIM = 32
QK_DIM = NOPE_DIM + ROPE_DIM
H_A = 8
Q_LORA = 384
KV_LORA = 256
ROPE_BASE = 10000.0
ATTN_SCALE = QK_DIM ** -0.5
LOG2E = math.log2(math.e)
N_R = 64
H_R = 8
RW_DIM = H_R * N_R
DECAY_LORA = 64
AAA_LORA = 64
GATE_LORA = 128
LNX_EPS = 64e-5
ATT_DIM = H_A * V_DIM
MLA_COLS = Q_LORA + KV_LORA + ROPE_DIM
RW_COLS = 3 * RW_DIM + DECAY_LORA + AAA_LORA + GATE_LORA
D_FF = 4 * D_MODEL
NORM_EPS = 1e-6
PAGE_SIZE = 128

LANES = 128
VMEM_LIMIT = 48 * 1024 * 1024
HEAD_PAD = LANES
QK_PAD = H_A * HEAD_PAD
IN_PAD = Q_LORA + KV_LORA + LANES + RW_COLS
ATT_BLK = 384
ATT_WIDE = 4
RW_CHUNK = 64
RW_STEP = 384
RW_GROUP = 6
FINISH_TILE = 512
PROMPT_TILE = 384

_NT = (((1,), (1,)), ((), ()))
_TN = (((0,), (0,)), ((), ()))
_NN = (((1,), (0,)), ((), ()))


def _dg(a, b, dims=_NN):
    return lax.dot_general(a, b, dims, preferred_element_type=F32)


def _dot(a, b, dims=_NN):
    return _dg(a.astype(BF16), b.astype(BF16), dims)


def _split(x):
    hi = x.astype(BF16)
    lo = (x - hi.astype(F32)).astype(BF16)
    return hi, lo


def _dot3(a, b, dims=_NN):
    ah, al = _split(a)
    bh, bl = _split(b)
    return _dg(ah, bh, dims) + (_dg(ah, bl, dims) + _dg(al, bh, dims))


def _rms(x, g):
    ms = jnp.mean(x * x, axis=-1, keepdims=True)
    return x * lax.rsqrt(ms + NORM_EPS) * g


def _const_spec(shape):
    nd = len(shape)
    return pl.BlockSpec(shape, lambda *_: (0,) * nd, pipeline_mode=pl.Buffered(1))


def _params(sem):
    return pltpu.CompilerParams(dimension_semantics=sem, vmem_limit_bytes=VMEM_LIMIT)


def _rope_chunk(x, tc, tm, tp):
    return x * tc + pltpu.roll(x, LANES - 16, 1) * tm + pltpu.roll(x, 16, 1) * tp


def _proj_kernel(h_ref, gpre_ref, win_ref, gcq_ref, gckv_ref, wuq_ref, wk_ref, wv_ref,
                 tc_ref, tm_ref, tp_ref, q_ref, k_ref, v_ref, ckv_ref, kr_ref, rw_ref):
    xn = _rms(h_ref[...], gpre_ref[...]).astype(BF16)
    tc, tm, tp = tc_ref[...], tm_ref[...], tp_ref[...]
    lane = lax.broadcasted_iota(jnp.int32, tc.shape, 1)

    cq = _rms(_dg(xn, win_ref[:, 0:Q_LORA]), gcq_ref[...])
    qf = _dot(cq, wuq_ref[...])
    for h in range(H_A):
        x = qf[:, h * HEAD_PAD:(h + 1) * HEAD_PAD]
        y = jnp.where(lane < NOPE_DIM, x, _rope_chunk(x, tc, tm, tp)) * (ATTN_SCALE * LOG2E)
        q_ref[:, h * HEAD_PAD:(h + 1) * HEAD_PAD] = y.astype(BF16)

    ckv = _rms(_dg(xn, win_ref[:, Q_LORA:Q_LORA + KV_LORA]), gckv_ref[...])
    ckv_ref[...] = ckv
    cb = ckv.astype(BF16)
    v_ref[...] = _dg(cb, wv_ref[...]).astype(BF16)
    kf = _dg(cb, wk_ref[...])
    kr0 = Q_LORA + KV_LORA
    ykr = _rope_chunk(_dg(xn, win_ref[:, kr0:kr0 + LANES]), tc, tm, tp)
    kr_ref[...] = ykr[:, 0:ROPE_DIM]
    kadd = jnp.where(lane >= NOPE_DIM, ykr, 0.0)
    for h in range(H_A):
        k_ref[:, h * HEAD_PAD:(h + 1) * HEAD_PAD] = (kf[:, h * HEAD_PAD:(h + 1) * HEAD_PAD] + kadd).astype(BF16)

    rw_ref[...] = _dg(xn, win_ref[:, kr0 + LANES:IN_PAD])


def _proj_call(h, lw, tables, tile, layer=None, prompt_out=None):
    rows = h.shape[0]
    nt = tables[0].shape[0] // tile
    nb = rows // (nt * tile)
    row = lambda n: pl.BlockSpec((tile, n), lambda b, t: (b * nt + t, 0))
    tab = pl.BlockSpec((tile, LANES), lambda b, t: (t, 0))
    kernel = _proj_kernel
    lat_specs = [row(KV_LORA), row(ROPE_DIM)]
    lat_shapes = [jax.ShapeDtypeStruct((rows, KV_LORA), F32), jax.ShapeDtypeStruct((rows, ROPE_DIM), F32)]
    extra, aliases = (), {}
    if prompt_out is not None:
        depth, acc = prompt_out
        lat_specs = [pl.BlockSpec((None, None, tile, n), lambda b, t: (layer, b, t, 0)) for n in (KV_LORA, ROPE_DIM)]
        lat_shapes = [jax.ShapeDtypeStruct((depth, nb, nt * tile, n), F32) for n in (KV_LORA, ROPE_DIM)]
        if acc is not None:
            extra = tuple(acc)
            aliases = {11: 3, 12: 4}
            kernel = lambda *refs: _proj_kernel(*refs[:11], *refs[13:])
    return pl.pallas_call(
        kernel,
        grid=(nb, nt),
        in_specs=[row(D_MODEL), _const_spec((1, D_MODEL)), _const_spec((D_MODEL, IN_PAD)),
                  _const_spec((1, Q_LORA)), _const_spec((1, KV_LORA)), _const_spec((Q_LORA, QK_PAD)),
                  _const_spec((KV_LORA, QK_PAD)), _const_spec((KV_LORA, ATT_DIM)), tab, tab, tab]
                 + [pl.BlockSpec(memory_space=pl.ANY)] * len(extra),
        out_specs=[row(QK_PAD), row(QK_PAD), row(ATT_DIM)] + lat_specs + [row(RW_COLS)],
        out_shape=[jax.ShapeDtypeStruct((rows, QK_PAD), BF16), jax.ShapeDtypeStruct((rows, QK_PAD), BF16),
                   jax.ShapeDtypeStruct((rows, ATT_DIM), BF16)] + lat_shapes
                  + [jax.ShapeDtypeStruct((rows, RW_COLS), F32)],
        input_output_aliases=aliases,
        compiler_params=_params(("parallel", "parallel")),
    )(h, lw["g_pre"], lw["w_in"], lw["g_cq"], lw["g_ckv"], lw["w_uq"], lw["w_k"], lw["w_v"], *tables, *extra)


def _attn_kernel(q_ref, k_ref, v_ref, o_ref):
    qi = pl.program_id(2)
    q = q_ref[...]

    def step(blk0, nblk, carry, masked=False):
        start = pl.multiple_of(blk0 * ATT_BLK, ATT_BLK)
        ks = k_ref[pl.ds(start, nblk * ATT_BLK), :]
        vs = v_ref[pl.ds(start, nblk * ATT_BLK), :]
        if masked:
            row = lax.broadcasted_iota(jnp.int32, (ATT_BLK, nblk * ATT_BLK), 0)
            col = lax.broadcasted_iota(jnp.int32, (ATT_BLK, nblk * ATT_BLK), 1)
            visible = col - (nblk - 1) * ATT_BLK <= row
        out = []
        for hh in range(2):
            m, l, acc = carry[hh]
            s = _dg(q[:, hh * HEAD_PAD:(hh + 1) * HEAD_PAD], ks[:, hh * HEAD_PAD:(hh + 1) * HEAD_PAD], _NT)
            if masked:
                s = jnp.where(visible, s, -jnp.inf)
            m_new = jnp.maximum(m, jnp.max(s, axis=-1, keepdims=True))
            alpha = jnp.exp2(m - m_new)
            p = jnp.exp2(s - m_new)
            l = alpha * l + jnp.sum(p, axis=-1, keepdims=True)
            acc = alpha * acc + _dg(p.astype(BF16), vs)
            out.append((m_new, l, acc))
        return tuple(out)

    one = (jnp.full((ATT_BLK, 1), -jnp.inf, F32), jnp.zeros((ATT_BLK, 1), F32),
           jnp.zeros((ATT_BLK, 2 * V_DIM), F32))
    wide = ATT_WIDE
    carry = lax.fori_loop(0, qi // wide, lambda j, c: step(j * wide, wide, c), (one, one))
    rest = qi % wide
    tails = [functools.partial(lambda n, c: step(qi - n, n + 1, c, masked=True), n) for n in range(wide)]
    (_, l0, a0), (_, l1, a1) = lax.switch(rest, tails, carry)
    lane = lax.broadcasted_iota(jnp.int32, (ATT_BLK, 2 * V_DIM), 1)
    o_ref[...] = jnp.where(lane < V_DIM, a0 / l0, a1 / l1).astype(BF16)


def _attn_call(q, k, v, batch, seq_pad):
    nq = seq_pad // ATT_BLK
    return pl.pallas_call(
        _attn_kernel,
        grid=(batch, H_A // 2, nq),
        in_specs=[pl.BlockSpec((ATT_BLK, 2 * HEAD_PAD), lambda b, hp, qi: (b * nq + qi, hp)),
                  pl.BlockSpec((seq_pad, 2 * HEAD_PAD), lambda b, hp, qi: (b, hp)),
                  pl.BlockSpec((seq_pad, 2 * V_DIM), lambda b, hp, qi: (b, hp))],
        out_specs=pl.BlockSpec((ATT_BLK, 2 * V_DIM), lambda b, hp, qi: (b * nq + qi, hp)),
        out_shape=jax.ShapeDtypeStruct((batch * seq_pad, ATT_DIM), BF16),
        compiler_params=_params(("parallel", "parallel", "arbitrary")),
    )(q, k, v)


def _rwkv_pre(x, prev, mu, w0, w2p, a0, a2p, g2, k_k, k_a, g8):
    xs = x + (prev - x) * mu
    r = xs[:, 0:RW_DIM]
    k = xs[:, RW_DIM:2 * RW_DIM]
    v = xs[:, 2 * RW_DIM:3 * RW_DIM]
    wa = xs[:, 3 * RW_DIM:3 * RW_DIM + LANES]
    gd = xs[:, 3 * RW_DIM + LANES:RW_COLS]
    z = w0 + _dot3(jnp.tanh(wa), w2p)
    lw = -math.exp(-0.5) * jax.nn.sigmoid(z)
    a = jax.nn.sigmoid(a0 + _dot(wa, a2p))
    g = _dot(jax.nn.sigmoid(gd), g2)
    kkr = k * k_k
    kk = kkr * lax.rsqrt(jnp.maximum(_dot(kkr * kkr, g8), 1e-24))
    k2 = k * (1.0 + (a - 1.0) * k_a)
    return r, k2, v, kk, a, lw, g


def _rwkv_post(y, r, k2, v, g, r_k, lnx_w, lnx_b, g8):
    mean = _dot(y, g8) * (1.0 / N_R)
    d = y - mean
    var = _dot(d * d, g8) * (1.0 / N_R)
    yn = d * lax.rsqrt(var + LNX_EPS) * lnx_w + lnx_b
    bonus = _dot(r * k2 * r_k, g8) * v
    return (yn + bonus) * g


def _rwkv_prompt_kernel(seq_real, rw_ref, mu_ref, w0_ref, w2_ref, a0_ref, a2_ref, g2_ref, kk_ref, ka_ref,
                        rk_ref, lnw_ref, lnb_ref, g8_ref, o_ref, st_ref,
                        st_scr, last_scr, rt_scr, at_scr, bt_scr, kt_scr, bw_scr, kw_scr, v_scr, wc_scr, y_scr):
    t = pl.program_id(1)
    C = RW_CHUNK

    @pl.when(t == 0)
    def _():
        st_scr[...] = jnp.zeros_like(st_scr)
        last_scr[...] = jnp.zeros_like(last_scr)

    x = rw_ref[...]
    rows = lax.broadcasted_iota(jnp.int32, (RW_STEP, 1), 0)
    prev = jnp.where(rows == 0, last_scr[...], pltpu.roll(x, 1, 0))
    last_scr[...] = x[RW_STEP - 1:RW_STEP, :]
    g8 = g8_ref[...]
    r, k2, v, kk, a, lw, g = _rwkv_pre(x, prev, mu_ref[...], w0_ref[...], w2_ref[...], a0_ref[...],
                                       a2_ref[...], g2_ref[...], kk_ref[...], ka_ref[...], g8)
    valid = (t * RW_STEP + rows) < seq_real
    lw = jnp.where(valid, lw, 0.0)
    kk = jnp.where(valid, kk, 0.0)
    k2m = jnp.where(valid, k2, 0.0)

    ri = lax.broadcasted_iota(jnp.int32, (LANES, LANES), 0)
    ci = lax.broadcasted_iota(jnp.int32, (LANES, LANES), 1)
    tri = jnp.where(((ri // C) == (ci // C)) & (ci <= ri), 1.0, 0.0).astype(BF16)
    lhi, llo = _split(lw)
    cum = jnp.concatenate([_dg(tri, lhi[i:i + LANES]) + _dg(tri, llo[i:i + LANES])
                           for i in range(0, RW_STEP, LANES)], 0)
    tot = jnp.concatenate([jnp.broadcast_to(cum[i + C - 1:i + C], (C, RW_DIM))
                           for i in range(0, RW_STEP, C)], 0)
    einv = jnp.exp(-cum)
    etail = jnp.exp(tot - cum)
    kb = kk * a
    rt_scr[...] = r * jnp.exp(cum)
    at_scr[...] = -kk * jnp.exp(cum - lw)
    bt_scr[...] = kb * einv
    kt_scr[...] = k2m * einv
    bw_scr[...] = kb * etail
    kw_scr[...] = k2m * etail
    v_scr[...] = v
    wc_scr[...] = jnp.exp(tot)

    r64 = lax.broadcasted_iota(jnp.int32, (C, C), 0)
    c64 = lax.broadcasted_iota(jnp.int32, (C, C), 1)
    strict = c64 < r64
    eye = c64 == r64
    r128 = lax.broadcasted_iota(jnp.int32, (C, 2 * C), 0)
    c128 = lax.broadcasted_iota(jnp.int32, (C, 2 * C), 1)
    incl2 = jnp.where(c128 >= C, c128 - C, c128) <= r128
    zeros64 = jnp.zeros((C, C), F32)

    def group(gi, carry):
        keys = [(cc, h) for cc in range(RW_GROUP) for h in range(H_R)]

        def rows(cc):
            return pl.ds(pl.multiple_of((gi * RW_GROUP + cc) * C, C), C)

        def ld(scr):
            return {(cc, h): scr[rows(cc), h * N_R:(h + 1) * N_R] for cc, h in keys}

        at, rt, v_ = ld(at_scr), ld(rt_scr), ld(v_scr)
        bt, kt = ld(bt_scr), ld(kt_scr)
        bk = {key: jnp.concatenate([bt[key], kt[key]], 0).astype(BF16) for key in keys}
        sc = {key: _dg(jnp.concatenate([at[key], rt[key]], 0).astype(BF16), bk[key], _NT) for key in keys}
        npow = {key: jnp.where(strict, sc[key][0:C, 0:C], 0.0).astype(BF16) for key in keys}
        a_ak = {key: jnp.where(strict, sc[key][0:C, C:2 * C], 0.0).astype(BF16) for key in keys}
        a_rbk = {key: jnp.where(incl2, sc[key][C:2 * C, :], 0.0).astype(BF16) for key in keys}
        vb = {key: v_[key].astype(BF16) for key in keys}
        xx = {key: jnp.concatenate([at[key], _dg(a_ak[key], vb[key])], 1) for key in keys}
        for j in range(6):
            xx = {key: xx[key] + _dg(npow[key], xx[key].astype(BF16)) for key in keys}
            if j < 5:
                npow = {key: _dg(npow[key], npow[key]).astype(BF16) for key in keys}
        zz = {key: jnp.concatenate([xx[key], jnp.concatenate([zeros64, v_[key]], 1)], 0).astype(BF16)
              for key in keys}
        yz = {key: _dg(a_rbk[key], zz[key]) for key in keys}
        bw, kw = ld(bw_scr), ld(kw_scr)
        bkw = {key: jnp.concatenate([bw[key], kw[key]], 0).astype(BF16) for key in keys}
        mz = {key: _dg(bkw[key], zz[key], _TN) for key in keys}
        wc = ld(wc_scr)
        st = [st_scr[h] for h in range(H_R)]
        for cc in range(RW_GROUP):
            stb = [s.astype(BF16) for s in st]
            for h in range(H_R):
                key = (cc, h)
                y_scr[rows(cc), h * N_R:(h + 1) * N_R] = (
                    _dg((rt[key] + yz[key][:, 0:C]).astype(BF16), stb[h]) + yz[key][:, C:2 * C])
                m = jnp.where(eye, wc[key], 0.0) + mz[key][:, 0:C]
                st[h] = _dg(m.astype(BF16), stb[h]) + mz[key][:, C:2 * C]
        for h in range(H_R):
            st_scr[h] = st[h]
        return carry

    lax.fori_loop(0, RW_STEP // C // RW_GROUP, group, 0)

    out = _rwkv_post(y_scr[...], r, k2, v, g, rk_ref[...], lnw_ref[...], lnb_ref[...], g8)
    o_ref[...] = out.astype(BF16)

    @pl.when(t == pl.num_programs(1) - 1)
    def _():
        st_ref[0] = st_scr[...]


def _rwkv_prompt_call(rw, lw, batch, seq_pad, seq_real):
    nt = seq_pad // RW_STEP
    vec = lambda n: _const_spec((1, n))
    big = pltpu.VMEM((RW_STEP, RW_DIM), F32)
    return pl.pallas_call(
        functools.partial(_rwkv_prompt_kernel, seq_real),
        grid=(batch, nt),
        in_specs=[pl.BlockSpec((RW_STEP, RW_COLS), lambda b, t: (b * nt + t, 0)),
                  vec(RW_COLS), vec(RW_DIM), _const_spec((LANES, RW_DIM)), vec(RW_DIM),
                  _const_spec((LANES, RW_DIM)), _const_spec((GATE_LORA, RW_DIM)), vec(RW_DIM), vec(RW_DIM),
                  vec(RW_DIM), vec(RW_DIM), vec(RW_DIM), _const_spec((RW_DIM, RW_DIM))],
        out_specs=[pl.BlockSpec((RW_STEP, RW_DIM), lambda b, t: (b * nt + t, 0)),
                   pl.BlockSpec((1, H_R, N_R, N_R), lambda b, t: (b, 0, 0, 0))],
        out_shape=[jax.ShapeDtypeStruct((batch * seq_pad, RW_DIM), BF16),
                   jax.ShapeDtypeStruct((batch, H_R, N_R, N_R), F32)],
        scratch_shapes=[pltpu.VMEM((H_R, N_R, N_R), F32), pltpu.VMEM((1, RW_COLS), F32),
                        big, big, big, big, big, big, big, big, big],
        compiler_params=_params(("arbitrary", "arbitrary")),
    )(rw, lw["mu"], lw["w0"], lw["w2"], lw["a0"], lw["a2"], lw["g2"], lw["k_k"], lw["k_a"],
      lw["r_k"], lw["lnx_w"], lw["lnx_b"], lw["g8"])


def _rwkv_step_kernel(rw_ref, sh_ref, s_ref, mu_ref, w0_ref, w2_ref, a0_ref, a2_ref, g2_ref, kk_ref, ka_ref,
                      rk_ref, lnw_ref, lnb_ref, g8_ref, o_ref, so_ref,
                      r_scr, k_scr, v_scr, g_scr, at_scr, bt_scr, wt_scr, kt_scr, vt_scr, rt_scr, yt_scr):
    h = pl.program_id(0)

    @pl.when(h == 0)
    def _():
        r, k2, v, kk, a, lw, g = _rwkv_pre(rw_ref[...], sh_ref[...], mu_ref[...], w0_ref[...], w2_ref[...],
                                           a0_ref[...], a2_ref[...], g2_ref[...], kk_ref[...], ka_ref[...],
                                           g8_ref[...])
        r_scr[...], k_scr[...], v_scr[...], g_scr[...] = r, k2, v, g
        at_scr[...] = (-kk).T
        bt_scr[...] = (kk * a).T
        wt_scr[...] = jnp.exp(lw).T
        kt_scr[...] = k2.T
        vt_scr[...] = v.T
        rt_scr[...] = r.T

    hrows = pl.ds(pl.multiple_of(h * N_R, N_R), N_R)
    a_h, b_h, w_h, k_h, r_h = at_scr[hrows, :], bt_scr[hrows, :], wt_scr[hrows, :], kt_scr[hrows, :], rt_scr[hrows, :]

    def vblock(vb, carry):
        v0 = pl.multiple_of(vb * 8, 8)
        vv = vt_scr[pl.ds(h * N_R + v0, 8), :]
        ys = []
        for i in range(8):
            s = s_ref[v0 + i]
            sa = jnp.sum(s * a_h, axis=0, keepdims=True)
            s2 = s * w_h + sa * b_h + vv[i:i + 1, :] * k_h
            so_ref[v0 + i] = s2
            ys.append(jnp.sum(s2 * r_h, axis=0, keepdims=True))
        yt_scr[pl.ds(h * N_R + v0, 8), :] = jnp.concatenate(ys, 0)
        return carry

    lax.fori_loop(0, N_R // 8, vblock, 0)

    @pl.when(h == pl.num_programs(0) - 1)
    def _():
        o_ref[...] = _rwkv_post(yt_scr[...].T, r_scr[...], k_scr[...], v_scr[...], g_scr[...], rk_ref[...],
                                lnw_ref[...], lnb_ref[...], g8_ref[...])


def _rwkv_step_call(rw, shift, state_t, layer, lw, acc):
    nb = rw.shape[0]
    vec = lambda n: _const_spec((1, n))
    rows = pltpu.VMEM((nb, RW_DIM), F32)
    cols = pltpu.VMEM((RW_DIM, nb), F32)
    depth = state_t.shape[0]
    kernel, extra, aliases = _rwkv_step_kernel, (), {}
    if acc is not None:
        kernel, extra, aliases = (lambda *refs: _rwkv_step_kernel(*refs[:15], *refs[16:])), (acc,), {15: 1}
    return pl.pallas_call(
        kernel,
        grid=(H_R,),
        in_specs=[_const_spec((nb, RW_COLS)), _const_spec((nb, RW_COLS)),
                  pl.BlockSpec((None, None, N_R, N_R, nb), lambda h: (layer, h, 0, 0, 0)),
                  vec(RW_COLS), vec(RW_DIM), _const_spec((LANES, RW_DIM)), vec(RW_DIM),
                  _const_spec((LANES, RW_DIM)), _const_spec((GATE_LORA, RW_DIM)), vec(RW_DIM), vec(RW_DIM),
                  vec(RW_DIM), vec(RW_DIM), vec(RW_DIM), _const_spec((RW_DIM, RW_DIM))]
                 + [pl.BlockSpec(memory_space=pl.ANY)] * len(extra),
        out_specs=[pl.BlockSpec((nb, RW_DIM), lambda h: (0, 0)),
                   pl.BlockSpec((None, None, N_R, N_R, nb), lambda h: (layer, h, 0, 0, 0))],
        out_shape=[jax.ShapeDtypeStruct((nb, RW_DIM), F32),
                   jax.ShapeDtypeStruct((depth, H_R, N_R, N_R, nb), F32)],
        input_output_aliases=aliases,
        scratch_shapes=[rows, rows, rows, rows, cols, cols, cols, cols, cols, cols, cols],
        compiler_params=_params(("arbitrary",)),
    )(rw, shift, state_t, lw["mu"], lw["w0"], lw["w2"], lw["a0"], lw["a2"], lw["g2"], lw["k_k"],
      lw["k_a"], lw["r_k"], lw["lnx_w"], lw["lnx_b"], lw["g8"], *extra)


def _finish_kernel(h_ref, oa_ref, orw_ref, woa_ref, wor_ref, gpost_ref, gfpre_ref, gfpost_ref,
                   wup_ref, wdn_ref, out_ref):
    o = _dot(oa_ref[...], woa_ref[...]) + _dot(orw_ref[...], wor_ref[...])
    h1 = h_ref[...] + _rms(o, gpost_ref[...])
    xn = _rms(h1, gfpre_ref[...]).astype(BF16)
    acc = jnp.zeros(h1.shape, F32)
    for c in range(D_FF // D_MODEL):
        cs = slice(c * D_MODEL, (c + 1) * D_MODEL)
        u = jnp.square(jnp.maximum(_dg(xn, wup_ref[:, cs]), 0.0))
        acc = acc + _dg(u.astype(BF16), wdn_ref[cs, :])
    out_ref[...] = h1 + _rms(acc, gfpost_ref[...])


def _finish_call(h, o_att, o_rw, lw, tile):
    rows = h.shape[0]
    row = lambda n: pl.BlockSpec((tile, n), lambda i: (i, 0))
    vec = _const_spec((1, D_MODEL))
    return pl.pallas_call(
        _finish_kernel,
        grid=(rows // tile,),
        in_specs=[row(D_MODEL), row(ATT_DIM), row(RW_DIM), _const_spec((ATT_DIM, D_MODEL)),
                  _const_spec((RW_DIM, D_MODEL)), vec, vec, vec,
                  _const_spec((D_MODEL, D_FF)), _const_spec((D_FF, D_MODEL))],
        out_specs=row(D_MODEL),
        out_shape=jax.ShapeDtypeStruct((rows, D_MODEL), F32),
        compiler_params=_params(("parallel",)),
    )(h, o_att, o_rw, lw["w_out_att"], lw["w_out_rw"], lw["g_post"], lw["g_ffn_pre"], lw["g_ffn_post"],
      lw["w_up"], lw["w_down"])


def _qlat_kernel(q_ref, wuk_ref, o_ref):
    q = q_ref[...]
    for h in range(H_A):
        o_ref[:, h * KV_LORA:(h + 1) * KV_LORA] = _dg(q[:, h * HEAD_PAD:(h + 1) * HEAD_PAD], wuk_ref[h])


def _qlat_call(q, w_uk):
    nb = q.shape[0]
    return pl.pallas_call(
        _qlat_kernel,
        out_shape=jax.ShapeDtypeStruct((nb, H_A * KV_LORA), F32),
    )(q, w_uk)


def _ouv_kernel(ol_ref, wuv_ref, o_ref):
    ol = ol_ref[...]
    acc = jnp.zeros(o_ref.shape, F32)
    for h in range(H_A):
        acc = acc + _dot(ol[:, h * KV_LORA:(h + 1) * KV_LORA], wuv_ref[h])
    o_ref[...] = acc


def _ouv_call(o_lat, w_uv):
    nb = o_lat.shape[0]
    return pl.pallas_call(
        _ouv_kernel,
        out_shape=jax.ShapeDtypeStruct((nb, ATT_DIM), F32),
    )(o_lat, w_uv)


def _decode_attn_kernel(layer, pt_ref, ql_ref, qr_ref, cn_ref, kn_ref, ckv_hbm, kr_hbm, o_ref, cbuf, kbuf, sem):
    b = pl.program_id(0)
    n_pages = cbuf.shape[1]
    slot = b % 2

    def page_copies(seq, sl, i):
        pid = pt_ref[seq, i]
        return (pltpu.make_async_copy(ckv_hbm.at[layer, pid], cbuf.at[sl, i], sem.at[0, sl]),
                pltpu.make_async_copy(kr_hbm.at[layer, pid],
                                      kbuf.at[sl, :, pl.ds(pl.multiple_of(i * PAGE_SIZE, PAGE_SIZE), PAGE_SIZE)],
                                      sem.at[1, sl]))

    def wait_all(seq, sl):
        def body(i, c):
            for cp in page_copies(seq, sl, i):
                cp.wait()
            return c
        lax.fori_loop(0, n_pages, body, 0)

    @pl.when(b == 0)
    def _():
        def body(i, c):
            for cp in page_copies(0, 0, i):
                cp.start()
            return c
        lax.fori_loop(0, n_pages, body, 0)

    wait_all(b, slot)
    nxt = jnp.minimum(b + 1, pl.num_programs(0) - 1)
    for i in range(n_pages):
        for cp in page_copies(nxt, 1 - slot, i):
            cp.start()

    ql = ql_ref[0]
    qr = qr_ref[0]
    cb = cbuf[slot].reshape(n_pages * PAGE_SIZE, KV_LORA).astype(BF16)
    s = _dg(ql.astype(BF16), cb, _NT) + _dg(qr.astype(BF16), kbuf[slot].astype(BF16))
    cn = cn_ref[0]
    kn = kn_ref[0]
    s_n = jnp.sum(ql * cn, axis=-1, keepdims=True) + jnp.sum(qr * kn, axis=-1, keepdims=True)
    m = jnp.maximum(jnp.max(s, axis=-1, keepdims=True), s_n)
    p = jnp.exp2(s - m)
    pn = jnp.exp2(s_n - m)
    pv = _dg(p.astype(BF16), cb) + pn * cn
    o_ref[0] = pv / (jnp.sum(p, axis=-1, keepdims=True) + pn)

    @pl.when(b == pl.num_programs(0) - 1)
    def _():
        wait_all(nxt, 1 - slot)


def _decode_attn_call(q_lat, q_rope, ckv_new, kr_new, cache_ckv, cache_krope_t, page_table, layer):
    nb, n_pages = page_table.shape
    per_b = lambda shape: pl.BlockSpec((1,) + shape, lambda b, pt: (b, 0, 0))
    hbm = pl.BlockSpec(memory_space=pl.ANY)
    grid_spec = pltpu.PrefetchScalarGridSpec(
        num_scalar_prefetch=1,
        grid=(nb,),
        in_specs=[per_b((H_A, KV_LORA)), per_b((H_A, ROPE_DIM)), per_b((1, KV_LORA)), per_b((1, ROPE_DIM)),
                  hbm, hbm],
        out_specs=per_b((H_A, KV_LORA)),
        scratch_shapes=[pltpu.VMEM((2, n_pages, PAGE_SIZE, KV_LORA), F32),
                        pltpu.VMEM((2, ROPE_DIM, n_pages * PAGE_SIZE), F32),
                        pltpu.SemaphoreType.DMA((2, 2))],
    )
    return pl.pallas_call(
        functools.partial(_decode_attn_kernel, layer),
        grid_spec=grid_spec,
        out_shape=jax.ShapeDtypeStruct((nb, H_A, KV_LORA), F32),
        compiler_params=_params(("arbitrary",)),
    )(page_table, q_lat, q_rope, ckv_new, kr_new, cache_ckv, cache_krope_t)


def _rope_tables(pos):
    half = ROPE_DIM // 2
    inv = ROPE_BASE ** (-jnp.arange(half, dtype=F32) / half)
    ang = pos.astype(F32)[:, None] * inv[None, :]
    cos, sin = jnp.cos(ang), jnp.sin(ang)
    z16, z32 = jnp.zeros_like(cos), jnp.zeros((pos.shape[0], ROPE_DIM), F32)
    seg_c = jnp.concatenate([cos, cos], 1)
    seg_m = jnp.concatenate([-sin, z16], 1)
    seg_p = jnp.concatenate([z16, sin], 1)
    lay = lambda seg: jnp.concatenate([seg, z32, seg, z32], 1)
    return lay(seg_c), lay(seg_m), lay(seg_p)


def _layer_weights(l, g_mix_pre, g_mix_post, g_ffn_pre, g_ffn_post, w_in, g_cq, g_ckv, w_uq, w_ukv, mu_shift,
                   w0, w2, a0, a2, g2, k_k, k_a, r_k, lnx_w, lnx_b, w_out, w_up, w_down, g8):
    row = lambda x: x[l].reshape(1, -1)
    wi = w_in[l]
    w_kr = wi[:, Q_LORA + KV_LORA:MLA_COLS]
    z = jnp.zeros_like(w_kr)
    w_in_p = jnp.concatenate([wi[:, :Q_LORA + KV_LORA], w_kr, z, w_kr, z, wi[:, MLA_COLS:]], 1).astype(BF16)
    w_uq_p = jnp.pad(w_uq[l].reshape(Q_LORA, H_A, QK_DIM), ((0, 0), (0, 0), (0, HEAD_PAD - QK_DIM)))
    wkv = w_ukv[l].reshape(KV_LORA, H_A, NOPE_DIM + V_DIM)
    w_uk, w_uv = wkv[..., :NOPE_DIM], wkv[..., NOPE_DIM:]
    w_k_p = jnp.pad(w_uk, ((0, 0), (0, 0), (0, HEAD_PAD - NOPE_DIM)))
    w_uk_t = jnp.pad(jnp.transpose(w_uk, (1, 2, 0)), ((0, 0), (0, HEAD_PAD - NOPE_DIM), (0, 0)))
    eye = jnp.eye(H_A, dtype=F32)
    w_uv_p = (jnp.transpose(w_uv, (1, 0, 2))[:, :, None, :] * eye[:, None, :, None]).reshape(H_A, KV_LORA, ATT_DIM)
    zl = jnp.zeros((DECAY_LORA, RW_DIM), F32)
    return {
        "g_pre": row(g_mix_pre), "g_post": row(g_mix_post), "g_ffn_pre": row(g_ffn_pre),
        "g_ffn_post": row(g_ffn_post), "g_cq": row(g_cq), "g_ckv": row(g_ckv),
        "w_in": w_in_p, "w_uq": w_uq_p.reshape(Q_LORA, QK_PAD).astype(BF16),
        "w_k": w_k_p.reshape(KV_LORA, QK_PAD).astype(BF16), "w_v": w_uv.reshape(KV_LORA, ATT_DIM).astype(BF16),
        "w_uk_t": w_uk_t.astype(BF16), "w_uv_p": w_uv_p.astype(BF16),
        "mu": row(mu_shift), "w0": row(w0), "a0": row(a0), "k_k": row(k_k), "k_a": row(k_a), "r_k": row(r_k),
        "lnx_w": row(lnx_w), "lnx_b": row(lnx_b),
        "w2": jnp.concatenate([w2[l], zl], 0), "a2": jnp.concatenate([zl, a2[l]], 0), "g2": g2[l], "g8": g8,
        "w_out_att": w_out[l, :ATT_DIM].astype(BF16), "w_out_rw": w_out[l, ATT_DIM:].astype(BF16),
        "w_up": w_up[l].astype(BF16), "w_down": w_down[l].astype(BF16),
    }


def kernel(x_prompt, x_sample, cache_ckv, cache_krope, state_wkv, state_shift, page_table, meta_tokens,
           g_mix_pre, g_mix_post, g_ffn_pre, g_ffn_post, w_in, g_cq, g_ckv, w_uq, w_ukv, mu_shift, w0, w2,
           a0, a2, g2, k_k, k_a, r_k, lnx_w, lnx_b, w_out, w_up, w_down):
    bp, seq, _ = x_prompt.shape
    bd, dec_seq, _ = x_sample.shape
    depth = w_in.shape[0]
    assert dec_seq == 1, "the decode kernels handle one new token per sequence"
    seq_real = seq + N_META
    seq_pad = -(-seq_real // ATT_BLK) * ATT_BLK
    assert seq_pad % RW_STEP == 0 and seq_pad % PROMPT_TILE == 0 and (bp * seq_pad) % FINISH_TILE == 0
    past_len = page_table.shape[1] * cache_ckv.shape[2]

    meta = jnp.broadcast_to(meta_tokens[None].astype(x_prompt.dtype), (bp, N_META, D_MODEL))
    h_p = jnp.pad(x_prompt, ((0, 0), (N_META, seq_pad - seq_real), (0, 0)))
    h_p = lax.dynamic_update_slice(h_p, meta, (0, 0, 0)).reshape(bp * seq_pad, D_MODEL)
    h_s = x_sample.reshape(bd, D_MODEL)
    tab_p = _rope_tables(jnp.arange(seq_pad))
    tab_s = _rope_tables(jnp.full((bd,), past_len, jnp.int32))
    hi = lax.broadcasted_iota(jnp.int32, (RW_DIM, RW_DIM), 0) // N_R
    hj = lax.broadcasted_iota(jnp.int32, (RW_DIM, RW_DIM), 1) // N_R
    g8 = (hi == hj).astype(BF16)
    cache_krope_t = jnp.swapaxes(cache_krope, 2, 3)
    state_t = jnp.transpose(state_wkv, (0, 2, 3, 4, 1))

    outs = [[] for _ in range(8)]
    p_lat, s_wkv = None, None
    for l in range(depth):
        lw = _layer_weights(l, g_mix_pre, g_mix_post, g_ffn_pre, g_ffn_post, w_in, g_cq, g_ckv, w_uq, w_ukv,
                            mu_shift, w0, w2, a0, a2, g2, k_k, k_a, r_k, lnx_w, lnx_b, w_out, w_up, w_down, g8)
        q, k, v, *p_lat, rw = _proj_call(h_p, lw, tab_p, PROMPT_TILE, l, (depth, p_lat))
        o_att = _attn_call(q, k, v, bp, seq_pad)
        o_rw, st = _rwkv_prompt_call(rw, lw, bp, seq_pad, seq_real)
        h_p = _finish_call(h_p, o_att, o_rw, lw, FINISH_TILE)
        outs[2].append(jnp.swapaxes(st, -1, -2))
        outs[3].append(rw.reshape(bp, seq_pad, RW_COLS)[:, seq_real - 1])
        q, _, _, ckv, kr, rw = _proj_call(h_s, lw, tab_s, bd)
        q_lat = _qlat_call(q, lw["w_uk_t"]).reshape(bd, H_A, KV_LORA)
        q_rope = q.reshape(bd, H_A, HEAD_PAD)[:, :, NOPE_DIM:QK_DIM].astype(F32)
        o_lat = _decode_attn_call(q_lat, q_rope, ckv.reshape(bd, 1, KV_LORA), kr.reshape(bd, 1, ROPE_DIM),
                                  cache_ckv, cache_krope_t, page_table, l)
        o_att = _ouv_call(o_lat.reshape(bd, H_A * KV_LORA), lw["w_uv_p"])
        o_rw, s_wkv = _rwkv_step_call(rw, state_shift[l], state_t, l, lw, s_wkv)
        h_s = _finish_call(h_s, o_att, o_rw, lw, bd)
        outs[4].append(ckv.reshape(bd, 1, KV_LORA))
        outs[5].append(kr.reshape(bd, 1, ROPE_DIM))
        outs[7].append(rw)

    y_prompt = h_p.reshape(bp, seq_pad, D_MODEL)[:, N_META:seq_real]
    y_sample = h_s.reshape(bd, 1, D_MODEL)
    outs = [jnp.stack(o) if o else None for o in outs]
    outs[0], outs[1] = (x[:, :, :seq_real] for x in p_lat)
    outs[6] = jnp.transpose(s_wkv, (0, 4, 1, 2, 3))
    return (y_prompt, y_sample) + tuple(outs)
```

```python
import functools
import math

import jax
import jax.numpy as jnp
from jax import lax
from jax.experimental import pallas as pl
from jax.experimental.pallas import tpu as pltpu

F32 = jnp.float32
BF16 = jnp.bfloat16

D_MODEL = 1024
N_META = 16
V_DIM = 64
NOPE_DIM = 64
ROPE_DIM = 32
QK_DIM = NOPE_DIM + ROPE_DIM
H_A = 8
Q_LORA = 384
KV_LORA = 256
ROPE_BASE = 10000.0
ATTN_SCALE = QK_DIM ** -0.5
LOG2E = math.log2(math.e)
N_R = 64
H_R = 8
RW_DIM = H_R * N_R
DECAY_LORA = 64
AAA_LORA = 64
GATE_LORA = 128
LNX_EPS = 64e-5
ATT_DIM = H_A * V_DIM
MLA_COLS = Q_LORA + KV_LORA + ROPE_DIM
RW_COLS = 3 * RW_DIM + DECAY_LORA + AAA_LORA + GATE_LORA
D_FF = 4 * D_MODEL
NORM_EPS = 1e-6
PAGE_SIZE = 128

LANES = 128
VMEM_LIMIT = 48 * 1024 * 1024
HEAD_PAD = LANES
QK_PAD = H_A * HEAD_PAD
IN_PAD = Q_LORA + KV_LORA + LANES + RW_COLS
ATT_BLK = 384
ATT_WIDE = 4
RW_CHUNK = 64
RW_STEP = 384
RW_GROUP = 6
FINISH_TILE = 512
PROMPT_TILE = 384

_NT = (((1,), (1,)), ((), ()))
_TN = (((0,), (0,)), ((), ()))
_NN = (((1,), (0,)), ((), ()))


def _dg(a, b, dims=_NN):
    return lax.dot_general(a, b, dims, preferred_element_type=F32)


def _dot(a, b, dims=_NN):
    return _dg(a.astype(BF16), b.astype(BF16), dims)


def _split(x):
    hi = x.astype(BF16)
    lo = (x - hi.astype(F32)).astype(BF16)
    return hi, lo


def _dot3(a, b, dims=_NN):
    ah, al = _split(a)
    bh, bl = _split(b)
    return _dg(ah, bh, dims) + (_dg(ah, bl, dims) + _dg(al, bh, dims))


def _rms(x, g):
    ms = jnp.mean(x * x, axis=-1, keepdims=True)
    return x * lax.rsqrt(ms + NORM_EPS) * g


def _const_spec(shape):
    nd = len(shape)
    return pl.BlockSpec(shape, lambda *_: (0,) * nd, pipeline_mode=pl.Buffered(1))


def _params(sem):
    return pltpu.CompilerParams(dimension_semantics=sem, vmem_limit_bytes=VMEM_LIMIT)


def _rope_chunk(x, tc, tm, tp):
    return x * tc + pltpu.roll(x, LANES - 16, 1) * tm + pltpu.roll(x, 16, 1) * tp


def _proj_kernel(h_ref, gpre_ref, win_ref, gcq_ref, gckv_ref, wuq_ref, wk_ref, wv_ref,
                 tc_ref, tm_ref, tp_ref, ckv_acc_ref, kr_acc_ref, q_ref, k_ref, v_ref, ckv_ref, kr_ref, rw_ref):
    del ckv_acc_ref, kr_acc_ref
    xn = _rms(h_ref[...], gpre_ref[...]).astype(BF16)
    tc, tm, tp = tc_ref[...], tm_ref[...], tp_ref[...]
    lane = lax.broadcasted_iota(jnp.int32, tc.shape, 1)

    cq = _rms(_dg(xn, win_ref[:, 0:Q_LORA]), gcq_ref[...])
    qf = _dot(cq, wuq_ref[...])
    for h in range(H_A):
        x = qf[:, h * HEAD_PAD:(h + 1) * HEAD_PAD]
        y = jnp.where(lane < NOPE_DIM, x, _rope_chunk(x, tc, tm, tp)) * (ATTN_SCALE * LOG2E)
        q_ref[:, h * HEAD_PAD:(h + 1) * HEAD_PAD] = y.astype(BF16)

    ckv = _rms(_dg(xn, win_ref[:, Q_LORA:Q_LORA + KV_LORA]), gckv_ref[...])
    ckv_ref[...] = ckv
    cb = ckv.astype(BF16)
    v_ref[...] = _dg(cb, wv_ref[...]).astype(BF16)
    kf = _dg(cb, wk_ref[...])
    kr0 = Q_LORA + KV_LORA
    ykr = _rope_chunk(_dg(xn, win_ref[:, kr0:kr0 + LANES]), tc, tm, tp)
    kr_ref[...] = ykr[:, 0:ROPE_DIM]
    kadd = jnp.where(lane >= NOPE_DIM, ykr, 0.0)
    for h in range(H_A):
        k_ref[:, h * HEAD_PAD:(h + 1) * HEAD_PAD] = (kf[:, h * HEAD_PAD:(h + 1) * HEAD_PAD] + kadd).astype(BF16)

    rw_ref[...] = _dg(xn, win_ref[:, kr0 + LANES:IN_PAD])


def _proj_call(h, lw, tables, tile, layer, acc):
    rows = h.shape[0]
    nt = tables[0].shape[0] // tile
    nb = rows // (nt * tile)
    row = lambda n: pl.BlockSpec((tile, n), lambda b, t: (b * nt + t, 0))
    tab = pl.BlockSpec((tile, LANES), lambda b, t: (t, 0))
    lat = lambda n: pl.BlockSpec((None, None, tile, n), lambda b, t: (layer, b, t, 0))
    hbm = pl.BlockSpec(memory_space=pl.ANY)
    return pl.pallas_call(
        _proj_kernel,
        grid=(nb, nt),
        in_specs=[row(D_MODEL), _const_spec((1, D_MODEL)), _const_spec((D_MODEL, IN_PAD)),
                  _const_spec((1, Q_LORA)), _const_spec((1, KV_LORA)), _const_spec((Q_LORA, QK_PAD)),
                  _const_spec((KV_LORA, QK_PAD)), _const_spec((KV_LORA, ATT_DIM)), tab, tab, tab, hbm, hbm],
        out_specs=[row(QK_PAD), row(QK_PAD), row(ATT_DIM), lat(KV_LORA), lat(ROPE_DIM), row(RW_COLS)],
        out_shape=[jax.ShapeDtypeStruct((rows, QK_PAD), BF16), jax.ShapeDtypeStruct((rows, QK_PAD), BF16),
                   jax.ShapeDtypeStruct((rows, ATT_DIM), BF16),
                   jax.ShapeDtypeStruct(acc[0].shape, F32), jax.ShapeDtypeStruct(acc[1].shape, F32),
                   jax.ShapeDtypeStruct((rows, RW_COLS), F32)],
        input_output_aliases={11: 3, 12: 4},
        compiler_params=_params(("parallel", "parallel")),
    )(h, lw["g_pre"], lw["w_in"], lw["g_cq"], lw["g_ckv"], lw["w_uq"], lw["w_k"], lw["w_v"], *tables, *acc)


def _attn_kernel(q_ref, k_ref, v_ref, o_ref):
    qi = pl.program_id(2)
    q = q_ref[...]

    def step(blk0, nblk, carry, masked=False):
        start = pl.multiple_of(blk0 * ATT_BLK, ATT_BLK)
        ks = k_ref[pl.ds(start, nblk * ATT_BLK), :]
        vs = v_ref[pl.ds(start, nblk * ATT_BLK), :]
        if masked:
            row = lax.broadcasted_iota(jnp.int32, (ATT_BLK, nblk * ATT_BLK), 0)
            col = lax.broadcasted_iota(jnp.int32, (ATT_BLK, nblk * ATT_BLK), 1)
            visible = col - (nblk - 1) * ATT_BLK <= row
        out = []
        for hh in range(2):
            m, l, acc = carry[hh]
            s = _dg(q[:, hh * HEAD_PAD:(hh + 1) * HEAD_PAD], ks[:, hh * HEAD_PAD:(hh + 1) * HEAD_PAD], _NT)
            if masked:
                s = jnp.where(visible, s, -jnp.inf)
            m_new = jnp.maximum(m, jnp.max(s, axis=-1, keepdims=True))
            alpha = jnp.exp2(m - m_new)
            p = jnp.exp2(s - m_new)
            l = alpha * l + jnp.sum(p, axis=-1, keepdims=True)
            acc = alpha * acc + _dg(p.astype(BF16), vs)
            out.append((m_new, l, acc))
        return tuple(out)

    one = (jnp.full((ATT_BLK, 1), -jnp.inf, F32), jnp.zeros((ATT_BLK, 1), F32),
           jnp.zeros((ATT_BLK, 2 * V_DIM), F32))
    wide = ATT_WIDE
    carry = lax.fori_loop(0, qi // wide, lambda j, c: step(j * wide, wide, c), (one, one))
    rest = qi % wide
    tails = [functools.partial(lambda n, c: step(qi - n, n + 1, c, masked=True), n) for n in range(wide)]
    (_, l0, a0), (_, l1, a1) = lax.switch(rest, tails, carry)
    lane = lax.broadcasted_iota(jnp.int32, (ATT_BLK, 2 * V_DIM), 1)
    o_ref[...] = jnp.where(lane < V_DIM, a0 / l0, a1 / l1).astype(BF16)


def _attn_call(q, k, v, batch, seq_pad):
    nq = seq_pad // ATT_BLK
    return pl.pallas_call(
        _attn_kernel,
        grid=(batch, H_A // 2, nq),
        in_specs=[pl.BlockSpec((ATT_BLK, 2 * HEAD_PAD), lambda b, hp, qi: (b * nq + qi, hp)),
                  pl.BlockSpec((seq_pad, 2 * HEAD_PAD), lambda b, hp, qi: (b, hp)),
                  pl.BlockSpec((seq_pad, 2 * V_DIM), lambda b, hp, qi: (b, hp))],
        out_specs=pl.BlockSpec((ATT_BLK, 2 * V_DIM), lambda b, hp, qi: (b * nq + qi, hp)),
        out_shape=jax.ShapeDtypeStruct((batch * seq_pad, ATT_DIM), BF16),
        compiler_params=_params(("parallel", "parallel", "arbitrary")),
    )(q, k, v)


def _rwkv_pre(x, prev, mu, w0, w2p, a0, a2p, g2, k_k, k_a, g8):
    xs = x + (prev - x) * mu
    r = xs[:, 0:RW_DIM]
    k = xs[:, RW_DIM:2 * RW_DIM]
    v = xs[:, 2 * RW_DIM:3 * RW_DIM]
    wa = xs[:, 3 * RW_DIM:3 * RW_DIM + LANES]
    gd = xs[:, 3 * RW_DIM + LANES:RW_COLS]
    z = w0 + _dot3(jnp.tanh(wa), w2p)
    lw = -math.exp(-0.5) * jax.nn.sigmoid(z)
    a = jax.nn.sigmoid(a0 + _dot(wa, a2p))
    g = _dot(jax.nn.sigmoid(gd), g2)
    kkr = k * k_k
    kk = kkr * lax.rsqrt(jnp.maximum(_dot(kkr * kkr, g8), 1e-24))
    k2 = k * (1.0 + (a - 1.0) * k_a)
    return r, k2, v, kk, a, lw, g


def _rwkv_post(y, r, k2, v, g, r_k, lnx_w, lnx_b, g8):
    mean = _dot(y, g8) * (1.0 / N_R)
    d = y - mean
    var = _dot(d * d, g8) * (1.0 / N_R)
    yn = d * lax.rsqrt(var + LNX_EPS) * lnx_w + lnx_b
    bonus = _dot(r * k2 * r_k, g8) * v
    return (yn + bonus) * g


def _rwkv_prompt_kernel(seq_real, rw_ref, mu_ref, w0_ref, w2_ref, a0_ref, a2_ref, g2_ref, kk_ref, ka_ref,
                        rk_ref, lnw_ref, lnb_ref, g8_ref, o_ref, st_ref,
                        st_scr, last_scr, rt_scr, at_scr, bt_scr, kt_scr, bw_scr, kw_scr, v_scr, wc_scr, y_scr):
    t = pl.program_id(1)
    C = RW_CHUNK

    @pl.when(t == 0)
    def _():
        st_scr[...] = jnp.zeros_like(st_scr)
        last_scr[...] = jnp.zeros_like(last_scr)

    x = rw_ref[...]
    rows = lax.broadcasted_iota(jnp.int32, (RW_STEP, 1), 0)
    prev = jnp.where(rows == 0, last_scr[...], pltpu.roll(x, 1, 0))
    last_scr[...] = x[RW_STEP - 1:RW_STEP, :]
    g8 = g8_ref[...]
    r, k2, v, kk, a, lw, g = _rwkv_pre(x, prev, mu_ref[...], w0_ref[...], w2_ref[...], a0_ref[...],
                                       a2_ref[...], g2_ref[...], kk_ref[...], ka_ref[...], g8)
    valid = (t * RW_STEP + rows) < seq_real
    lw = jnp.where(valid, lw, 0.0)
    kk = jnp.where(valid, kk, 0.0)
    k2m = jnp.where(valid, k2, 0.0)

    ri = lax.broadcasted_iota(jnp.int32, (LANES, LANES), 0)
    ci = lax.broadcasted_iota(jnp.int32, (LANES, LANES), 1)
    tri = jnp.where(((ri // C) == (ci // C)) & (ci <= ri), 1.0, 0.0).astype(BF16)
    lhi, llo = _split(lw)
    cum = jnp.concatenate([_dg(tri, lhi[i:i + LANES]) + _dg(tri, llo[i:i + LANES])
                           for i in range(0, RW_STEP, LANES)], 0)
    tot = jnp.concatenate([jnp.broadcast_to(cum[i + C - 1:i + C], (C, RW_DIM))
                           for i in range(0, RW_STEP, C)], 0)
    einv = jnp.exp(-cum)
    etail = jnp.exp(tot - cum)
    kb = kk * a
    rt_scr[...] = r * jnp.exp(cum)
    at_scr[...] = -kk * jnp.exp(cum - lw)
    bt_scr[...] = kb * einv
    kt_scr[...] = k2m * einv
    bw_scr[...] = kb * etail
    kw_scr[...] = k2m * etail
    v_scr[...] = v
    wc_scr[...] = jnp.exp(tot)

    r64 = lax.broadcasted_iota(jnp.int32, (C, C), 0)
    c64 = lax.broadcasted_iota(jnp.int32, (C, C), 1)
    strict = c64 < r64
    eye = c64 == r64
    r128 = lax.broadcasted_iota(jnp.int32, (C, 2 * C), 0)
    c128 = lax.broadcasted_iota(jnp.int32, (C, 2 * C), 1)
    incl2 = jnp.where(c128 >= C, c128 - C, c128) <= r128
    zeros64 = jnp.zeros((C, C), F32)

    def group(gi, carry):
        keys = [(cc, h) for cc in range(RW_GROUP) for h in range(H_R)]

        def rows(cc):
            return pl.ds(pl.multiple_of((gi * RW_GROUP + cc) * C, C), C)

        def ld(scr):
            return {(cc, h): scr[rows(cc), h * N_R:(h + 1) * N_R] for cc, h in keys}

        at, rt, v_ = ld(at_scr), ld(rt_scr), ld(v_scr)
        bt, kt = ld(bt_scr), ld(kt_scr)
        bk = {key: jnp.concatenate([bt[key], kt[key]], 0).astype(BF16) for key in keys}
        sc = {key: _dg(jnp.concatenate([at[key], rt[key]], 0).astype(BF16), bk[key], _NT) for key in keys}
        npow = {key: jnp.where(strict, sc[key][0:C, 0:C], 0.0).astype(BF16) for key in keys}
        a_ak = {key: jnp.where(strict, sc[key][0:C, C:2 * C], 0.0).astype(BF16) for key in keys}
        a_rbk = {key: jnp.where(incl2, sc[key][C:2 * C, :], 0.0).astype(BF16) for key in keys}
        vb = {key: v_[key].astype(BF16) for key in keys}
        xx = {key: jnp.concatenate([at[key], _dg(a_ak[key], vb[key])], 1) for key in keys}
        for j in range(6):
            xx = {key: xx[key] + _dg(npow[key], xx[key].astype(BF16)) for key in keys}
            if j < 5:
                npow = {key: _dg(npow[key], npow[key]).astype(BF16) for key in keys}
        zz = {key: jnp.concatenate([xx[key], jnp.concatenate([zeros64, v_[key]], 1)], 0).astype(BF16)
              for key in keys}
        yz = {key: _dg(a_rbk[key], zz[key]) for key in keys}
        bw, kw = ld(bw_scr), ld(kw_scr)
        bkw = {key: jnp.concatenate([bw[key], kw[key]], 0).astype(BF16) for key in keys}
        mz = {key: _dg(bkw[key], zz[key], _TN) for key in keys}
        wc = ld(wc_scr)
        st = [st_scr[h] for h in range(H_R)]
        for cc in range(RW_GROUP):
            stb = [s.astype(BF16) for s in st]
            for h in range(H_R):
                key = (cc, h)
                y_scr[rows(cc), h * N_R:(h + 1) * N_R] = (
                    _dg((rt[key] + yz[key][:, 0:C]).astype(BF16), stb[h]) + yz[key][:, C:2 * C])
                m = jnp.where(eye, wc[key], 0.0) + mz[key][:, 0:C]
                st[h] = _dg(m.astype(BF16), stb[h]) + mz[key][:, C:2 * C]
        for h in range(H_R):
            st_scr[h] = st[h]
        return carry

    lax.fori_loop(0, RW_STEP // C // RW_GROUP, group, 0)

    out = _rwkv_post(y_scr[...], r, k2, v, g, rk_ref[...], lnw_ref[...], lnb_ref[...], g8)
    o_ref[...] = out.astype(BF16)

    @pl.when(t == pl.num_programs(1) - 1)
    def _():
        st_ref[0] = st_scr[...]


def _rwkv_prompt_call(rw, lw, batch, seq_pad, seq_real):
    nt = seq_pad // RW_STEP
    vec = lambda n: _const_spec((1, n))
    big = pltpu.VMEM((RW_STEP, RW_DIM), F32)
    return pl.pallas_call(
        functools.partial(_rwkv_prompt_kernel, seq_real),
        grid=(batch, nt),
        in_specs=[pl.BlockSpec((RW_STEP, RW_COLS), lambda b, t: (b * nt + t, 0)),
                  vec(RW_COLS), vec(RW_DIM), _const_spec((LANES, RW_DIM)), vec(RW_DIM),
                  _const_spec((LANES, RW_DIM)), _const_spec((GATE_LORA, RW_DIM)), vec(RW_DIM), vec(RW_DIM),
                  vec(RW_DIM), vec(RW_DIM), vec(RW_DIM), _const_spec((RW_DIM, RW_DIM))],
        out_specs=[pl.BlockSpec((RW_STEP, RW_DIM), lambda b, t: (b * nt + t, 0)),
                   pl.BlockSpec((1, H_R, N_R, N_R), lambda b, t: (b, 0, 0, 0))],
        out_shape=[jax.ShapeDtypeStruct((batch * seq_pad, RW_DIM), BF16),
                   jax.ShapeDtypeStruct((batch, H_R, N_R, N_R), F32)],
        scratch_shapes=[pltpu.VMEM((H_R, N_R, N_R), F32), pltpu.VMEM((1, RW_COLS), F32),
                        big, big, big, big, big, big, big, big, big],
        compiler_params=_params(("arbitrary", "arbitrary")),
    )(rw, lw["mu"], lw["w0"], lw["w2"], lw["a0"], lw["a2"], lw["g2"], lw["k_k"], lw["k_a"],
      lw["r_k"], lw["lnx_w"], lw["lnx_b"], lw["g8"])


def _rwkv_step_kernel(rw_ref, sh_ref, s_ref, mu_ref, w0_ref, w2_ref, a0_ref, a2_ref, g2_ref, kk_ref, ka_ref,
                      rk_ref, lnw_ref, lnb_ref, g8_ref, acc_ref, o_ref, so_ref,
                      r_scr, k_scr, v_scr, g_scr, at_scr, bt_scr, wt_scr, kt_scr, vt_scr, rt_scr, yt_scr):
    del acc_ref
    h = pl.program_id(0)

    @pl.when(h == 0)
    def _():
        r, k2, v, kk, a, lw, g = _rwkv_pre(rw_ref[...], sh_ref[...], mu_ref[...], w0_ref[...], w2_ref[...],
                                           a0_ref[...], a2_ref[...], g2_ref[...], kk_ref[...], ka_ref[...],
                                           g8_ref[...])
        r_scr[...], k_scr[...], v_scr[...], g_scr[...] = r, k2, v, g
        at_scr[...] = (-kk).T
        bt_scr[...] = (kk * a).T
        wt_scr[...] = jnp.exp(lw).T
        kt_scr[...] = k2.T
        vt_scr[...] = v.T
        rt_scr[...] = r.T

    hrows = pl.ds(pl.multiple_of(h * N_R, N_R), N_R)
    a_h, b_h, w_h, k_h, r_h = at_scr[hrows, :], bt_scr[hrows, :], wt_scr[hrows, :], kt_scr[hrows, :], rt_scr[hrows, :]

    def vblock(vb, carry):
        v0 = pl.multiple_of(vb * 8, 8)
        vv = vt_scr[pl.ds(h * N_R + v0, 8), :]
        ys = []
        for i in range(8):
            s = s_ref[v0 + i]
            sa = jnp.sum(s * a_h, axis=0, keepdims=True)
            s2 = s * w_h + sa * b_h + vv[i:i + 1, :] * k_h
            so_ref[v0 + i] = s2
            ys.append(jnp.sum(s2 * r_h, axis=0, keepdims=True))
        yt_scr[pl.ds(h * N_R + v0, 8), :] = jnp.concatenate(ys, 0)
        return carry

    lax.fori_loop(0, N_R // 8, vblock, 0)

    @pl.when(h == pl.num_programs(0) - 1)
    def _():
        o_ref[...] = _rwkv_post(yt_scr[...].T, r_scr[...], k_scr[...], v_scr[...], g_scr[...], rk_ref[...],
                                lnw_ref[...], lnb_ref[...], g8_ref[...])


def _rwkv_step_call(rw, shift, state_t, layer, lw, acc):
    nb = rw.shape[0]
    vec = lambda n: _const_spec((1, n))
    rows = pltpu.VMEM((nb, RW_DIM), F32)
    cols = pltpu.VMEM((RW_DIM, nb), F32)
    return pl.pallas_call(
        _rwkv_step_kernel,
        grid=(H_R,),
        in_specs=[_const_spec((nb, RW_COLS)), _const_spec((nb, RW_COLS)),
                  pl.BlockSpec((None, None, N_R, N_R, nb), lambda h: (layer, h, 0, 0, 0)),
                  vec(RW_COLS), vec(RW_DIM), _const_spec((LANES, RW_DIM)), vec(RW_DIM),
                  _const_spec((LANES, RW_DIM)), _const_spec((GATE_LORA, RW_DIM)), vec(RW_DIM), vec(RW_DIM),
                  vec(RW_DIM), vec(RW_DIM), vec(RW_DIM), _const_spec((RW_DIM, RW_DIM))]
                 + [pl.BlockSpec(memory_space=pl.ANY)],
        out_specs=[pl.BlockSpec((nb, RW_DIM), lambda h: (0, 0)),
                   pl.BlockSpec((None, None, N_R, N_R, nb), lambda h: (layer, h, 0, 0, 0))],
        out_shape=[jax.ShapeDtypeStruct((nb, RW_DIM), F32),
                   jax.ShapeDtypeStruct(acc.shape, F32)],
        input_output_aliases={15: 1},
        scratch_shapes=[rows, rows, rows, rows, cols, cols, cols, cols, cols, cols, cols],
        compiler_params=_params(("arbitrary",)),
    )(rw, shift, state_t, lw["mu"], lw["w0"], lw["w2"], lw["a0"], lw["a2"], lw["g2"], lw["k_k"],
      lw["k_a"], lw["r_k"], lw["lnx_w"], lw["lnx_b"], lw["g8"], acc)


def _finish_kernel(h_ref, oa_ref, orw_ref, woa_ref, wor_ref, gpost_ref, gfpre_ref, gfpost_ref,
                   wup_ref, wdn_ref, out_ref):
    o = _dot(oa_ref[...], woa_ref[...]) + _dot(orw_ref[...], wor_ref[...])
    h1 = h_ref[...] + _rms(o, gpost_ref[...])
    xn = _rms(h1, gfpre_ref[...]).astype(BF16)
    acc = jnp.zeros(h1.shape, F32)
    for c in range(D_FF // D_MODEL):
        cs = slice(c * D_MODEL, (c + 1) * D_MODEL)
        u = jnp.square(jnp.maximum(_dg(xn, wup_ref[:, cs]), 0.0))
        acc = acc + _dg(u.astype(BF16), wdn_ref[cs, :])
    out_ref[...] = h1 + _rms(acc, gfpost_ref[...])


def _finish_call(h, o_att, o_rw, lw, tile):
    rows = h.shape[0]
    row = lambda n: pl.BlockSpec((tile, n), lambda i: (i, 0))
    vec = _const_spec((1, D_MODEL))
    return pl.pallas_call(
        _finish_kernel,
        grid=(rows // tile,),
        in_specs=[row(D_MODEL), row(ATT_DIM), row(RW_DIM), _const_spec((ATT_DIM, D_MODEL)),
                  _const_spec((RW_DIM, D_MODEL)), vec, vec, vec,
                  _const_spec((D_MODEL, D_FF)), _const_spec((D_FF, D_MODEL))],
        out_specs=row(D_MODEL),
        out_shape=jax.ShapeDtypeStruct((rows, D_MODEL), F32),
        compiler_params=_params(("parallel",)),
    )(h, o_att, o_rw, lw["w_out_att"], lw["w_out_rw"], lw["g_post"], lw["g_ffn_pre"], lw["g_ffn_post"],
      lw["w_up"], lw["w_down"])


def _qlat_kernel(q_ref, wuk_ref, o_ref):
    q = q_ref[...]
    for h in range(H_A):
        o_ref[:, h * KV_LORA:(h + 1) * KV_LORA] = _dg(q[:, h * HEAD_PAD:(h + 1) * HEAD_PAD], wuk_ref[h])


def _qlat_call(q, w_uk):
    nb = q.shape[0]
    return pl.pallas_call(
        _qlat_kernel,
        out_shape=jax.ShapeDtypeStruct((nb, H_A * KV_LORA), F32),
    )(q, w_uk)


def _ouv_kernel(ol_ref, wuv_ref, o_ref):
    ol = ol_ref[...]
    acc = jnp.zeros(o_ref.shape, F32)
    for h in range(H_A):
        acc = acc + _dot(ol[:, h * KV_LORA:(h + 1) * KV_LORA], wuv_ref[h])
    o_ref[...] = acc


def _ouv_call(o_lat, w_uv):
    nb = o_lat.shape[0]
    return pl.pallas_call(
        _ouv_kernel,
        out_shape=jax.ShapeDtypeStruct((nb, ATT_DIM), F32),
    )(o_lat, w_uv)


def _decode_attn_kernel(layer, pt_ref, ql_ref, qr_ref, cn_ref, kn_ref, ckv_hbm, kr_hbm, o_ref, cbuf, kbuf, sem):
    b = pl.program_id(0)
    n_pages = cbuf.shape[1]
    slot = b % 2

    def page_copies(seq, sl, i):
        pid = pt_ref[seq, i]
        return (pltpu.make_async_copy(ckv_hbm.at[layer, pid], cbuf.at[sl, i], sem.at[0, sl]),
                pltpu.make_async_copy(kr_hbm.at[layer, pid],
                                      kbuf.at[sl, :, pl.ds(pl.multiple_of(i * PAGE_SIZE, PAGE_SIZE), PAGE_SIZE)],
                                      sem.at[1, sl]))

    def wait_all(seq, sl):
        def body(i, c):
            for cp in page_copies(seq, sl, i):
                cp.wait()
            return c
        lax.fori_loop(0, n_pages, body, 0)

    @pl.when(b == 0)
    def _():
        def body(i, c):
            for cp in page_copies(0, 0, i):
                cp.start()
            return c
        lax.fori_loop(0, n_pages, body, 0)

    wait_all(b, slot)
    nxt = jnp.minimum(b + 1, pl.num_programs(0) - 1)
    for i in range(n_pages):
        for cp in page_copies(nxt, 1 - slot, i):
            cp.start()

    ql = ql_ref[0]
    qr = qr_ref[0]
    cb = cbuf[slot].reshape(n_pages * PAGE_SIZE, KV_LORA).astype(BF16)
    s = _dg(ql.astype(BF16), cb, _NT) + _dg(qr.astype(BF16), kbuf[slot].astype(BF16))
    cn = cn_ref[0]
    kn = kn_ref[0]
    s_n = jnp.sum(ql * cn, axis=-1, keepdims=True) + jnp.sum(qr * kn, axis=-1, keepdims=True)
    m = jnp.maximum(jnp.max(s, axis=-1, keepdims=True), s_n)
    p = jnp.exp2(s - m)
    pn = jnp.exp2(s_n - m)
    pv = _dg(p.astype(BF16), cb) + pn * cn
    o_ref[0] = pv / (jnp.sum(p, axis=-1, keepdims=True) + pn)

    @pl.when(b == pl.num_programs(0) - 1)
    def _():
        wait_all(nxt, 1 - slot)


def _decode_attn_call(q_lat, q_rope, ckv_new, kr_new, cache_ckv, cache_krope_t, page_table, layer):
    nb, n_pages = page_table.shape
    per_b = lambda shape: pl.BlockSpec((1,) + shape, lambda b, pt: (b, 0, 0))
    hbm = pl.BlockSpec(memory_space=pl.ANY)
    grid_spec = pltpu.PrefetchScalarGridSpec(
        num_scalar_prefetch=1,
        grid=(nb,),
        in_specs=[per_b((H_A, KV_LORA)), per_b((H_A, ROPE_DIM)), per_b((1, KV_LORA)), per_b((1, ROPE_DIM)),
                  hbm, hbm],
        out_specs=per_b((H_A, KV_LORA)),
        scratch_shapes=[pltpu.VMEM((2, n_pages, PAGE_SIZE, KV_LORA), F32),
                        pltpu.VMEM((2, ROPE_DIM, n_pages * PAGE_SIZE), F32),
                        pltpu.SemaphoreType.DMA((2, 2))],
    )
    return pl.pallas_call(
        functools.partial(_decode_attn_kernel, layer),
        grid_spec=grid_spec,
        out_shape=jax.ShapeDtypeStruct((nb, H_A, KV_LORA), F32),
        compiler_params=_params(("arbitrary",)),
    )(page_table, q_lat, q_rope, ckv_new, kr_new, cache_ckv, cache_krope_t)


def _rope_tables(pos):
    half = ROPE_DIM // 2
    inv = ROPE_BASE ** (-jnp.arange(half, dtype=F32) / half)
    ang = pos.astype(F32)[:, None] * inv[None, :]
    cos, sin = jnp.cos(ang), jnp.sin(ang)
    z16, z32 = jnp.zeros_like(cos), jnp.zeros((pos.shape[0], ROPE_DIM), F32)
    seg_c = jnp.concatenate([cos, cos], 1)
    seg_m = jnp.concatenate([-sin, z16], 1)
    seg_p = jnp.concatenate([z16, sin], 1)
    lay = lambda seg: jnp.concatenate([seg, z32, seg, z32], 1)
    return lay(seg_c), lay(seg_m), lay(seg_p)


def _layer_weights(l, g_mix_pre, g_mix_post, g_ffn_pre, g_ffn_post, w_in, g_cq, g_ckv, w_uq, w_ukv, mu_shift,
                   w0, w2, a0, a2, g2, k_k, k_a, r_k, lnx_w, lnx_b, w_out, w_up, w_down, g8):
    row = lambda x: x[l].reshape(1, -1)
    wi = w_in[l]
    w_kr = wi[:, Q_LORA + KV_LORA:MLA_COLS]
    z = jnp.zeros_like(w_kr)
    w_in_p = jnp.concatenate([wi[:, :Q_LORA + KV_LORA], w_kr, z, w_kr, z, wi[:, MLA_COLS:]], 1).astype(BF16)
    w_uq_p = jnp.pad(w_uq[l].reshape(Q_LORA, H_A, QK_DIM), ((0, 0), (0, 0), (0, HEAD_PAD - QK_DIM)))
    wkv = w_ukv[l].reshape(KV_LORA, H_A, NOPE_DIM + V_DIM)
    w_uk, w_uv = wkv[..., :NOPE_DIM], wkv[..., NOPE_DIM:]
    w_k_p = jnp.pad(w_uk, ((0, 0), (0, 0), (0, HEAD_PAD - NOPE_DIM)))
    w_uk_t = jnp.pad(jnp.transpose(w_uk, (1, 2, 0)), ((0, 0), (0, HEAD_PAD - NOPE_DIM), (0, 0)))
    eye = jnp.eye(H_A, dtype=F32)
    w_uv_p = (jnp.transpose(w_uv, (1, 0, 2))[:, :, None, :] * eye[:, None, :, None]).reshape(H_A, KV_LORA, ATT_DIM)
    zl = jnp.zeros((DECAY_LORA, RW_DIM), F32)
    return {
        "g_pre": row(g_mix_pre), "g_post": row(g_mix_post), "g_ffn_pre": row(g_ffn_pre),
        "g_ffn_post": row(g_ffn_post), "g_cq": row(g_cq), "g_ckv": row(g_ckv),
        "w_in": w_in_p, "w_uq": w_uq_p.reshape(Q_LORA, QK_PAD).astype(BF16),
        "w_k": w_k_p.reshape(KV_LORA, QK_PAD).astype(BF16), "w_v": w_uv.reshape(KV_LORA, ATT_DIM).astype(BF16),
        "w_uk_t": w_uk_t.astype(BF16), "w_uv_p": w_uv_p.astype(BF16),
        "mu": row(mu_shift), "w0": row(w0), "a0": row(a0), "k_k": row(k_k), "k_a": row(k_a), "r_k": row(r_k),
        "lnx_w": row(lnx_w), "lnx_b": row(lnx_b),
        "w2": jnp.concatenate([w2[l], zl], 0), "a2": jnp.concatenate([zl, a2[l]], 0), "g2": g2[l], "g8": g8,
        "w_out_att": w_out[l, :ATT_DIM].astype(BF16), "w_out_rw": w_out[l, ATT_DIM:].astype(BF16),
        "w_up": w_up[l].astype(BF16), "w_down": w_down[l].astype(BF16),
    }


def kernel(x_prompt, x_sample, cache_ckv, cache_krope, state_wkv, state_shift, page_table, meta_tokens,
           g_mix_pre, g_mix_post, g_ffn_pre, g_ffn_post, w_in, g_cq, g_ckv, w_uq, w_ukv, mu_shift, w0, w2,
           a0, a2, g2, k_k, k_a, r_k, lnx_w, lnx_b, w_out, w_up, w_down):
    bp, seq, _ = x_prompt.shape
    bd, dec_seq, _ = x_sample.shape
    depth = w_in.shape[0]
    assert dec_seq == 1, "the decode kernels handle one new token per sequence"
    seq_real = seq + N_META
    seq_pad = -(-seq_real // ATT_BLK) * ATT_BLK
    assert seq_pad % RW_STEP == 0 and seq_pad % PROMPT_TILE == 0 and (bp * seq_pad) % FINISH_TILE == 0
    past_len = page_table.shape[1] * cache_ckv.shape[2]

    meta = jnp.broadcast_to(meta_tokens[None].astype(x_prompt.dtype), (bp, N_META, D_MODEL))
    h_p = jnp.pad(x_prompt, ((0, 0), (N_META, seq_pad - seq_real), (0, 0)))
    h_p = lax.dynamic_update_slice(h_p, meta, (0, 0, 0)).reshape(bp * seq_pad, D_MODEL)
    h_s = x_sample.reshape(bd, D_MODEL)
    tab_p = _rope_tables(jnp.arange(seq_pad))
    tab_s = _rope_tables(jnp.full((bd,), past_len, jnp.int32))
    hi = lax.broadcasted_iota(jnp.int32, (RW_DIM, RW_DIM), 0) // N_R
    hj = lax.broadcasted_iota(jnp.int32, (RW_DIM, RW_DIM), 1) // N_R
    g8 = (hi == hj).astype(BF16)
    cache_krope_t = jnp.swapaxes(cache_krope, 2, 3)
    state_t = jnp.transpose(state_wkv, (0, 2, 3, 4, 1))

    outs = [[] for _ in range(8)]
    p_lat = [jnp.zeros((depth, bp, seq_pad, n), F32) for n in (KV_LORA, ROPE_DIM)]
    s_lat = [jnp.zeros((depth, 1, bd, n), F32) for n in (KV_LORA, ROPE_DIM)]
    s_wkv = jnp.zeros(state_t.shape, F32)
    for l in range(depth):
        lw = _layer_weights(l, g_mix_pre, g_mix_post, g_ffn_pre, g_ffn_post, w_in, g_cq, g_ckv, w_uq, w_ukv,
                            mu_shift, w0, w2, a0, a2, g2, k_k, k_a, r_k, lnx_w, lnx_b, w_out, w_up, w_down, g8)
        q, k, v, *p_lat, rw = _proj_call(h_p, lw, tab_p, PROMPT_TILE, l, p_lat)
        o_att = _attn_call(q, k, v, bp, seq_pad)
        o_rw, st = _rwkv_prompt_call(rw, lw, bp, seq_pad, seq_real)
        h_p = _finish_call(h_p, o_att, o_rw, lw, FINISH_TILE)
        outs[2].append(jnp.swapaxes(st, -1, -2))
        outs[3].append(rw.reshape(bp, seq_pad, RW_COLS)[:, seq_real - 1])
        q, _, _, *s_lat, rw = _proj_call(h_s, lw, tab_s, bd, l, s_lat)
        q_lat = _qlat_call(q, lw["w_uk_t"]).reshape(bd, H_A, KV_LORA)
        q_rope = q.reshape(bd, H_A, HEAD_PAD)[:, :, NOPE_DIM:QK_DIM].astype(F32)
        o_lat = _decode_attn_call(q_lat, q_rope, s_lat[0][l].reshape(bd, 1, KV_LORA), s_lat[1][l].reshape(bd, 1, ROPE_DIM),
                                  cache_ckv, cache_krope_t, page_table, l)
        o_att = _ouv_call(o_lat.reshape(bd, H_A * KV_LORA), lw["w_uv_p"])
        o_rw, s_wkv = _rwkv_step_call(rw, state_shift[l], state_t, l, lw, s_wkv)
        h_s = _finish_call(h_s, o_att, o_rw, lw, bd)
        outs[7].append(rw)

    y_prompt = h_p.reshape(bp, seq_pad, D_MODEL)[:, N_META:seq_real]
    y_sample = h_s.reshape(bd, 1, D_MODEL)
    outs = [jnp.stack(o) if o else None for o in outs]
    outs[0], outs[1] = (x[:, :, :seq_real] for x in p_lat)
    outs[4], outs[5] = (x.reshape(depth, bd, 1, -1) for x in s_lat)
    outs[6] = jnp.transpose(s_wkv, (0, 4, 1, 2, 3))
    return (y_prompt, y_sample) + tuple(outs)
```

```python
import functools
import math

import jax
import jax.numpy as jnp
from jax import lax
from jax.experimental import pallas as pl
from jax.experimental.pallas import tpu as pltpu

F32 = jnp.float32
BF16 = jnp.bfloat16

D_MODEL = 1024
N_META = 16
V_DIM = 64
NOPE_DIM = 64
ROPE_DIM = 32
QK_DIM = NOPE_DIM + ROPE_DIM
H_A = 8
Q_LORA = 384
KV_LORA = 256
ROPE_BASE = 10000.0
ATTN_SCALE = QK_DIM ** -0.5
LOG2E = math.log2(math.e)
N_R = 64
H_R = 8
RW_DIM = H_R * N_R
DECAY_LORA = 64
AAA_LORA = 64
GATE_LORA = 128
LNX_EPS = 64e-5
ATT_DIM = H_A * V_DIM
MLA_COLS = Q_LORA + KV_LORA + ROPE_DIM
RW_COLS = 3 * RW_DIM + DECAY_LORA + AAA_LORA + GATE_LORA
D_FF = 4 * D_MODEL
NORM_EPS = 1e-6
PAGE_SIZE = 128

LANES = 128
VMEM_LIMIT = 48 * 1024 * 1024
HEAD_PAD = LANES
QK_PAD = H_A * HEAD_PAD
IN_PAD = Q_LORA + KV_LORA + LANES + RW_COLS
ATT_BLK = 384
ATT_WIDE = 8
RW_CHUNK = 64
RW_STEP = 384
RW_GROUP = 6
FINISH_TILE = 512
PROMPT_TILE = 384

_NT = (((1,), (1,)), ((), ()))
_TN = (((0,), (0,)), ((), ()))
_NN = (((1,), (0,)), ((), ()))


def _dg(a, b, dims=_NN):
    return lax.dot_general(a, b, dims, preferred_element_type=F32)


def _dot(a, b, dims=_NN):
    return _dg(a.astype(BF16), b.astype(BF16), dims)


def _split(x):
    hi = x.astype(BF16)
    lo = (x - hi.astype(F32)).astype(BF16)
    return hi, lo


def _dot3(a, b, dims=_NN):
    ah, al = _split(a)
    bh, bl = _split(b)
    return _dg(ah, bh, dims) + (_dg(ah, bl, dims) + _dg(al, bh, dims))


def _rms(x, g):
    ms = jnp.mean(x * x, axis=-1, keepdims=True)
    return x * lax.rsqrt(ms + NORM_EPS) * g


def _const_spec(shape):
    nd = len(shape)
    return pl.BlockSpec(shape, lambda *_: (0,) * nd, pipeline_mode=pl.Buffered(1))


def _params(sem):
    return pltpu.CompilerParams(dimension_semantics=sem, vmem_limit_bytes=VMEM_LIMIT)


def _rope_chunk(x, tc, tm, tp):
    return x * tc + pltpu.roll(x, LANES - 16, 1) * tm + pltpu.roll(x, 16, 1) * tp


def _proj_kernel(h_ref, gpre_ref, win_ref, gcq_ref, gckv_ref, wuq_ref, wk_ref, wv_ref,
                 tc_ref, tm_ref, tp_ref, ckv_acc_ref, kr_acc_ref, q_ref, k_ref, v_ref, ckv_ref, kr_ref, rw_ref):
    del ckv_acc_ref, kr_acc_ref
    xn = _rms(h_ref[...], gpre_ref[...]).astype(BF16)
    tc, tm, tp = tc_ref[...], tm_ref[...], tp_ref[...]
    lane = lax.broadcasted_iota(jnp.int32, tc.shape, 1)

    cq = _rms(_dg(xn, win_ref[:, 0:Q_LORA]), gcq_ref[...])
    qf = _dot(cq, wuq_ref[...])
    for h in range(H_A):
        x = qf[:, h * HEAD_PAD:(h + 1) * HEAD_PAD]
        y = jnp.where(lane < NOPE_DIM, x, _rope_chunk(x, tc, tm, tp)) * (ATTN_SCALE * LOG2E)
        q_ref[:, h * HEAD_PAD:(h + 1) * HEAD_PAD] = y.astype(BF16)

    ckv = _rms(_dg(xn, win_ref[:, Q_LORA:Q_LORA + KV_LORA]), gckv_ref[...])
    ckv_ref[...] = ckv
    cb = ckv.astype(BF16)
    v_ref[...] = _dg(cb, wv_ref[...]).astype(BF16)
    kf = _dg(cb, wk_ref[...])
    kr0 = Q_LORA + KV_LORA
    ykr = _rope_chunk(_dg(xn, win_ref[:, kr0:kr0 + LANES]), tc, tm, tp)
    kr_ref[...] = ykr[:, 0:ROPE_DIM]
    kadd = jnp.where(lane >= NOPE_DIM, ykr, 0.0)
    for h in range(H_A):
        k_ref[:, h * HEAD_PAD:(h + 1) * HEAD_PAD] = (kf[:, h * HEAD_PAD:(h + 1) * HEAD_PAD] + kadd).astype(BF16)

    rw_ref[...] = _dg(xn, win_ref[:, kr0 + LANES:IN_PAD])


def _proj_call(h, lw, tables, tile, layer, acc):
    rows = h.shape[0]
    nt = tables[0].shape[0] // tile
    nb = rows // (nt * tile)
    row = lambda n: pl.BlockSpec((tile, n), lambda b, t: (b * nt + t, 0))
    tab = pl.BlockSpec((tile, LANES), lambda b, t: (t, 0))
    lat = lambda n: pl.BlockSpec((None, None, tile, n), lambda b, t: (layer, b, t, 0))
    hbm = pl.BlockSpec(memory_space=pl.ANY)
    return pl.pallas_call(
        _proj_kernel,
        grid=(nb, nt),
        in_specs=[row(D_MODEL), _const_spec((1, D_MODEL)), _const_spec((D_MODEL, IN_PAD)),
                  _const_spec((1, Q_LORA)), _const_spec((1, KV_LORA)), _const_spec((Q_LORA, QK_PAD)),
                  _const_spec((KV_LORA, QK_PAD)), _const_spec((KV_LORA, ATT_DIM)), tab, tab, tab, hbm, hbm],
        out_specs=[row(QK_PAD), row(QK_PAD), row(ATT_DIM), lat(KV_LORA), lat(ROPE_DIM), row(RW_COLS)],
        out_shape=[jax.ShapeDtypeStruct((rows, QK_PAD), BF16), jax.ShapeDtypeStruct((rows, QK_PAD), BF16),
                   jax.ShapeDtypeStruct((rows, ATT_DIM), BF16),
                   jax.ShapeDtypeStruct(acc[0].shape, F32), jax.ShapeDtypeStruct(acc[1].shape, F32),
                   jax.ShapeDtypeStruct((rows, RW_COLS), F32)],
        input_output_aliases={11: 3, 12: 4},
        compiler_params=_params(("parallel", "parallel")),
    )(h, lw["g_pre"], lw["w_in"], lw["g_cq"], lw["g_ckv"], lw["w_uq"], lw["w_k"], lw["w_v"], *tables, *acc)


def _attn_kernel(q_ref, k_ref, v_ref, o_ref):
    qi = pl.program_id(2)
    q = q_ref[...]

    def step(blk0, nblk, carry, masked=False):
        start = pl.multiple_of(blk0 * ATT_BLK, ATT_BLK)
        ks = k_ref[pl.ds(start, nblk * ATT_BLK), :]
        vs = v_ref[pl.ds(start, nblk * ATT_BLK), :]
        if masked:
            row = lax.broadcasted_iota(jnp.int32, (ATT_BLK, nblk * ATT_BLK), 0)
            col = lax.broadcasted_iota(jnp.int32, (ATT_BLK, nblk * ATT_BLK), 1)
            visible = col - (nblk - 1) * ATT_BLK <= row
        out = []
        for hh in range(2):
            m, l, acc = carry[hh]
            s = _dg(q[:, hh * HEAD_PAD:(hh + 1) * HEAD_PAD], ks[:, hh * HEAD_PAD:(hh + 1) * HEAD_PAD], _NT)
            if masked:
                s = jnp.where(visible, s, -jnp.inf)
            m_new = jnp.maximum(m, jnp.max(s, axis=-1, keepdims=True))
            alpha = jnp.exp2(m - m_new)
            p = jnp.exp2(s - m_new)
            l = alpha * l + jnp.sum(p, axis=-1, keepdims=True)
            acc = alpha * acc + _dg(p.astype(BF16), vs)
            out.append((m_new, l, acc))
        return tuple(out)

    one = (jnp.full((ATT_BLK, 1), -jnp.inf, F32), jnp.zeros((ATT_BLK, 1), F32),
           jnp.zeros((ATT_BLK, 2 * V_DIM), F32))
    wide = ATT_WIDE
    carry = lax.fori_loop(0, qi // wide, lambda j, c: step(j * wide, wide, c), (one, one))
    rest = qi % wide
    tails = [functools.partial(lambda n, c: step(qi - n, n + 1, c, masked=True), n) for n in range(wide)]
    (_, l0, a0), (_, l1, a1) = lax.switch(rest, tails, carry)
    lane = lax.broadcasted_iota(jnp.int32, (ATT_BLK, 2 * V_DIM), 1)
    o_ref[...] = jnp.where(lane < V_DIM, a0 / l0, a1 / l1).astype(BF16)


def _attn_call(q, k, v, batch, seq_pad):
    nq = seq_pad // ATT_BLK
    return pl.pallas_call(
        _attn_kernel,
        grid=(batch, H_A // 2, nq),
        in_specs=[pl.BlockSpec((ATT_BLK, 2 * HEAD_PAD), lambda b, hp, qi: (b * nq + qi, hp)),
                  pl.BlockSpec((seq_pad, 2 * HEAD_PAD), lambda b, hp, qi: (b, hp)),
                  pl.BlockSpec((seq_pad, 2 * V_DIM), lambda b, hp, qi: (b, hp))],
        out_specs=pl.BlockSpec((ATT_BLK, 2 * V_DIM), lambda b, hp, qi: (b * nq + qi, hp)),
        out_shape=jax.ShapeDtypeStruct((batch * seq_pad, ATT_DIM), BF16),
        compiler_params=_params(("parallel", "parallel", "arbitrary")),
    )(q, k, v)


def _rwkv_pre(x, prev, mu, w0, w2p, a0, a2p, g2, k_k, k_a, g8):
    xs = x + (prev - x) * mu
    r = xs[:, 0:RW_DIM]
    k = xs[:, RW_DIM:2 * RW_DIM]
    v = xs[:, 2 * RW_DIM:3 * RW_DIM]
    wa = xs[:, 3 * RW_DIM:3 * RW_DIM + LANES]
    gd = xs[:, 3 * RW_DIM + LANES:RW_COLS]
    z = w0 + _dot3(jnp.tanh(wa), w2p)
    lw = -math.exp(-0.5) * jax.nn.sigmoid(z)
    a = jax.nn.sigmoid(a0 + _dot(wa, a2p))
    g = _dot(jax.nn.sigmoid(gd), g2)
    kkr = k * k_k
    kk = kkr * lax.rsqrt(jnp.maximum(_dot(kkr * kkr, g8), 1e-24))
    k2 = k * (1.0 + (a - 1.0) * k_a)
    return r, k2, v, kk, a, lw, g


def _rwkv_post(y, r, k2, v, g, r_k, lnx_w, lnx_b, g8):
    mean = _dot(y, g8) * (1.0 / N_R)
    d = y - mean
    var = _dot(d * d, g8) * (1.0 / N_R)
    yn = d * lax.rsqrt(var + LNX_EPS) * lnx_w + lnx_b
    bonus = _dot(r * k2 * r_k, g8) * v
    return (yn + bonus) * g


def _rwkv_prompt_kernel(seq_real, rw_ref, mu_ref, w0_ref, w2_ref, a0_ref, a2_ref, g2_ref, kk_ref, ka_ref,
                        rk_ref, lnw_ref, lnb_ref, g8_ref, o_ref, st_ref,
                        st_scr, last_scr, rt_scr, at_scr, bt_scr, kt_scr, bw_scr, kw_scr, v_scr, wc_scr, y_scr):
    t = pl.program_id(1)
    C = RW_CHUNK

    @pl.when(t == 0)
    def _():
        st_scr[...] = jnp.zeros_like(st_scr)
        last_scr[...] = jnp.zeros_like(last_scr)

    x = rw_ref[...]
    rows = lax.broadcasted_iota(jnp.int32, (RW_STEP, 1), 0)
    prev = jnp.where(rows == 0, last_scr[...], pltpu.roll(x, 1, 0))
    last_scr[...] = x[RW_STEP - 1:RW_STEP, :]
    g8 = g8_ref[...]
    r, k2, v, kk, a, lw, g = _rwkv_pre(x, prev, mu_ref[...], w0_ref[...], w2_ref[...], a0_ref[...],
                                       a2_ref[...], g2_ref[...], kk_ref[...], ka_ref[...], g8)
    valid = (t * RW_STEP + rows) < seq_real
    lw = jnp.where(valid, lw, 0.0)
    kk = jnp.where(valid, kk, 0.0)
    k2m = jnp.where(valid, k2, 0.0)

    ri = lax.broadcasted_iota(jnp.int32, (LANES, LANES), 0)
    ci = lax.broadcasted_iota(jnp.int32, (LANES, LANES), 1)
    tri = jnp.where(((ri // C) == (ci // C)) & (ci <= ri), 1.0, 0.0).astype(BF16)
    lhi, llo = _split(lw)
    cum = jnp.concatenate([_dg(tri, lhi[i:i + LANES]) + _dg(tri, llo[i:i + LANES])
                           for i in range(0, RW_STEP, LANES)], 0)
    tot = jnp.concatenate([jnp.broadcast_to(cum[i + C - 1:i + C], (C, RW_DIM))
                           for i in range(0, RW_STEP, C)], 0)
    einv = jnp.exp(-cum)
    etail = jnp.exp(tot - cum)
    kb = kk * a
    rt_scr[...] = r * jnp.exp(cum)
    at_scr[...] = -kk * jnp.exp(cum - lw)
    bt_scr[...] = kb * einv
    kt_scr[...] = k2m * einv
    bw_scr[...] = kb * etail
    kw_scr[...] = k2m * etail
    v_scr[...] = v
    wc_scr[...] = jnp.exp(tot)

    r64 = lax.broadcasted_iota(jnp.int32, (C, C), 0)
    c64 = lax.broadcasted_iota(jnp.int32, (C, C), 1)
    strict = c64 < r64
    eye = c64 == r64
    r128 = lax.broadcasted_iota(jnp.int32, (C, 2 * C), 0)
    c128 = lax.broadcasted_iota(jnp.int32, (C, 2 * C), 1)
    incl2 = jnp.where(c128 >= C, c128 - C, c128) <= r128
    zeros64 = jnp.zeros((C, C), F32)

    def group(gi, carry):
        keys = [(cc, h) for cc in range(RW_GROUP) for h in range(H_R)]

        def rows(cc):
            return pl.ds(pl.multiple_of((gi * RW_GROUP + cc) * C, C), C)

        def ld(scr):
            return {(cc, h): scr[rows(cc), h * N_R:(h + 1) * N_R] for cc, h in keys}

        at, rt, v_ = ld(at_scr), ld(rt_scr), ld(v_scr)
        bt, kt = ld(bt_scr), ld(kt_scr)
        bk = {key: jnp.concatenate([bt[key], kt[key]], 0).astype(BF16) for key in keys}
        sc = {key: _dg(jnp.concatenate([at[key], rt[key]], 0).astype(BF16), bk[key], _NT) for key in keys}
        npow = {key: jnp.where(strict, sc[key][0:C, 0:C], 0.0).astype(BF16) for key in keys}
        a_ak = {key: jnp.where(strict, sc[key][0:C, C:2 * C], 0.0).astype(BF16) for key in keys}
        a_rbk = {key: jnp.where(incl2, sc[key][C:2 * C, :], 0.0).astype(BF16) for key in keys}
        vb = {key: v_[key].astype(BF16) for key in keys}
        xx = {key: jnp.concatenate([at[key], _dg(a_ak[key], vb[key])], 1) for key in keys}
        for j in range(6):
            xx = {key: xx[key] + _dg(npow[key], xx[key].astype(BF16)) for key in keys}
            if j < 5:
                npow = {key: _dg(npow[key], npow[key]).astype(BF16) for key in keys}
        zz = {key: jnp.concatenate([xx[key], jnp.concatenate([zeros64, v_[key]], 1)], 0).astype(BF16)
              for key in keys}
        yz = {key: _dg(a_rbk[key], zz[key]) for key in keys}
        bw, kw = ld(bw_scr), ld(kw_scr)
        bkw = {key: jnp.concatenate([bw[key], kw[key]], 0).astype(BF16) for key in keys}
        mz = {key: _dg(bkw[key], zz[key], _TN) for key in keys}
        wc = ld(wc_scr)
        st = [st_scr[h] for h in range(H_R)]
        for cc in range(RW_GROUP):
            stb = [s.astype(BF16) for s in st]
            for h in range(H_R):
                key = (cc, h)
                y_scr[rows(cc), h * N_R:(h + 1) * N_R] = (
                    _dg((rt[key] + yz[key][:, 0:C]).astype(BF16), stb[h]) + yz[key][:, C:2 * C])
                m = jnp.where(eye, wc[key], 0.0) + mz[key][:, 0:C]
                st[h] = _dg(m.astype(BF16), stb[h]) + mz[key][:, C:2 * C]
        for h in range(H_R):
            st_scr[h] = st[h]
        return carry

    lax.fori_loop(0, RW_STEP // C // RW_GROUP, group, 0)

    out = _rwkv_post(y_scr[...], r, k2, v, g, rk_ref[...], lnw_ref[...], lnb_ref[...], g8)
    o_ref[...] = out.astype(BF16)

    @pl.when(t == pl.num_programs(1) - 1)
    def _():
        st_ref[0] = st_scr[...]


def _rwkv_prompt_call(rw, lw, batch, seq_pad, seq_real):
    nt = seq_pad // RW_STEP
    vec = lambda n: _const_spec((1, n))
    big = pltpu.VMEM((RW_STEP, RW_DIM), F32)
    return pl.pallas_call(
        functools.partial(_rwkv_prompt_kernel, seq_real),
        grid=(batch, nt),
        in_specs=[pl.BlockSpec((RW_STEP, RW_COLS), lambda b, t: (b * nt + t, 0)),
                  vec(RW_COLS), vec(RW_DIM), _const_spec((LANES, RW_DIM)), vec(RW_DIM),
                  _const_spec((LANES, RW_DIM)), _const_spec((GATE_LORA, RW_DIM)), vec(RW_DIM), vec(RW_DIM),
                  vec(RW_DIM), vec(RW_DIM), vec(RW_DIM), _const_spec((RW_DIM, RW_DIM))],
        out_specs=[pl.BlockSpec((RW_STEP, RW_DIM), lambda b, t: (b * nt + t, 0)),
                   pl.BlockSpec((1, H_R, N_R, N_R), lambda b, t: (b, 0, 0, 0))],
        out_shape=[jax.ShapeDtypeStruct((batch * seq_pad, RW_DIM), BF16),
                   jax.ShapeDtypeStruct((batch, H_R, N_R, N_R), F32)],
        scratch_shapes=[pltpu.VMEM((H_R, N_R, N_R), F32), pltpu.VMEM((1, RW_COLS), F32),
                        big, big, big, big, big, big, big, big, big],
        compiler_params=_params(("arbitrary", "arbitrary")),
    )(rw, lw["mu"], lw["w0"], lw["w2"], lw["a0"], lw["a2"], lw["g2"], lw["k_k"], lw["k_a"],
      lw["r_k"], lw["lnx_w"], lw["lnx_b"], lw["g8"])


def _rwkv_step_kernel(rw_ref, sh_ref, s_ref, mu_ref, w0_ref, w2_ref, a0_ref, a2_ref, g2_ref, kk_ref, ka_ref,
                      rk_ref, lnw_ref, lnb_ref, g8_ref, acc_ref, o_ref, so_ref,
                      r_scr, k_scr, v_scr, g_scr, at_scr, bt_scr, wt_scr, kt_scr, vt_scr, rt_scr, yt_scr):
    del acc_ref
    h = pl.program_id(0)

    @pl.when(h == 0)
    def _():
        r, k2, v, kk, a, lw, g = _rwkv_pre(rw_ref[...], sh_ref[...], mu_ref[...], w0_ref[...], w2_ref[...],
                                           a0_ref[...], a2_ref[...], g2_ref[...], kk_ref[...], ka_ref[...],
                                           g8_ref[...])
        r_scr[...], k_scr[...], v_scr[...], g_scr[...] = r, k2, v, g
        at_scr[...] = (-kk).T
        bt_scr[...] = (kk * a).T
        wt_scr[...] = jnp.exp(lw).T
        kt_scr[...] = k2.T
        vt_scr[...] = v.T
        rt_scr[...] = r.T

    hrows = pl.ds(pl.multiple_of(h * N_R, N_R), N_R)
    a_h, b_h, w_h, k_h, r_h = at_scr[hrows, :], bt_scr[hrows, :], wt_scr[hrows, :], kt_scr[hrows, :], rt_scr[hrows, :]

    def vblock(vb, carry):
        v0 = pl.multiple_of(vb * 8, 8)
        vv = vt_scr[pl.ds(h * N_R + v0, 8), :]
        ys = []
        for i in range(8):
            s = s_ref[v0 + i]
            sa = jnp.sum(s * a_h, axis=0, keepdims=True)
            s2 = s * w_h + sa * b_h + vv[i:i + 1, :] * k_h
            so_ref[v0 + i] = s2
            ys.append(jnp.sum(s2 * r_h, axis=0, keepdims=True))
        yt_scr[pl.ds(h * N_R + v0, 8), :] = jnp.concatenate(ys, 0)
        return carry

    lax.fori_loop(0, N_R // 8, vblock, 0)

    @pl.when(h == pl.num_programs(0) - 1)
    def _():
        o_ref[...] = _rwkv_post(yt_scr[...].T, r_scr[...], k_scr[...], v_scr[...], g_scr[...], rk_ref[...],
                                lnw_ref[...], lnb_ref[...], g8_ref[...])


def _rwkv_step_call(rw, shift, state_t, layer, lw, acc):
    nb = rw.shape[0]
    vec = lambda n: _const_spec((1, n))
    rows = pltpu.VMEM((nb, RW_DIM), F32)
    cols = pltpu.VMEM((RW_DIM, nb), F32)
    return pl.pallas_call(
        _rwkv_step_kernel,
        grid=(H_R,),
        in_specs=[_const_spec((nb, RW_COLS)), _const_spec((nb, RW_COLS)),
                  pl.BlockSpec((None, None, N_R, N_R, nb), lambda h: (layer, h, 0, 0, 0)),
                  vec(RW_COLS), vec(RW_DIM), _const_spec((LANES, RW_DIM)), vec(RW_DIM),
                  _const_spec((LANES, RW_DIM)), _const_spec((GATE_LORA, RW_DIM)), vec(RW_DIM), vec(RW_DIM),
                  vec(RW_DIM), vec(RW_DIM), vec(RW_DIM), _const_spec((RW_DIM, RW_DIM))]
                 + [pl.BlockSpec(memory_space=pl.ANY)],
        out_specs=[pl.BlockSpec((nb, RW_DIM), lambda h: (0, 0)),
                   pl.BlockSpec((None, None, N_R, N_R, nb), lambda h: (layer, h, 0, 0, 0))],
        out_shape=[jax.ShapeDtypeStruct((nb, RW_DIM), F32),
                   jax.ShapeDtypeStruct(acc.shape, F32)],
        input_output_aliases={15: 1},
        scratch_shapes=[rows, rows, rows, rows, cols, cols, cols, cols, cols, cols, cols],
        compiler_params=_params(("arbitrary",)),
    )(rw, shift, state_t, lw["mu"], lw["w0"], lw["w2"], lw["a0"], lw["a2"], lw["g2"], lw["k_k"],
      lw["k_a"], lw["r_k"], lw["lnx_w"], lw["lnx_b"], lw["g8"], acc)


def _finish_kernel(h_ref, oa_ref, orw_ref, woa_ref, wor_ref, gpost_ref, gfpre_ref, gfpost_ref,
                   wup_ref, wdn_ref, out_ref):
    o = _dot(oa_ref[...], woa_ref[...]) + _dot(orw_ref[...], wor_ref[...])
    h1 = h_ref[...] + _rms(o, gpost_ref[...])
    xn = _rms(h1, gfpre_ref[...]).astype(BF16)
    acc = jnp.zeros(h1.shape, F32)
    for c in range(D_FF // D_MODEL):
        cs = slice(c * D_MODEL, (c + 1) * D_MODEL)
        u = jnp.square(jnp.maximum(_dg(xn, wup_ref[:, cs]), 0.0))
        acc = acc + _dg(u.astype(BF16), wdn_ref[cs, :])
    out_ref[...] = h1 + _rms(acc, gfpost_ref[...])


def _finish_call(h, o_att, o_rw, lw, tile):
    rows = h.shape[0]
    row = lambda n: pl.BlockSpec((tile, n), lambda i: (i, 0))
    vec = _const_spec((1, D_MODEL))
    return pl.pallas_call(
        _finish_kernel,
        grid=(rows // tile,),
        in_specs=[row(D_MODEL), row(ATT_DIM), row(RW_DIM), _const_spec((ATT_DIM, D_MODEL)),
                  _const_spec((RW_DIM, D_MODEL)), vec, vec, vec,
                  _const_spec((D_MODEL, D_FF)), _const_spec((D_FF, D_MODEL))],
        out_specs=row(D_MODEL),
        out_shape=jax.ShapeDtypeStruct((rows, D_MODEL), F32),
        compiler_params=_params(("parallel",)),
    )(h, o_att, o_rw, lw["w_out_att"], lw["w_out_rw"], lw["g_post"], lw["g_ffn_pre"], lw["g_ffn_post"],
      lw["w_up"], lw["w_down"])


def _qlat_kernel(q_ref, wuk_ref, o_ref):
    q = q_ref[...]
    for h in range(H_A):
        o_ref[:, h * KV_LORA:(h + 1) * KV_LORA] = _dg(q[:, h * HEAD_PAD:(h + 1) * HEAD_PAD], wuk_ref[h])


def _qlat_call(q, w_uk):
    nb = q.shape[0]
    return pl.pallas_call(
        _qlat_kernel,
        out_shape=jax.ShapeDtypeStruct((nb, H_A * KV_LORA), F32),
    )(q, w_uk)


def _ouv_kernel(ol_ref, wuv_ref, o_ref):
    ol = ol_ref[...]
    acc = jnp.zeros(o_ref.shape, F32)
    for h in range(H_A):
        acc = acc + _dot(ol[:, h * KV_LORA:(h + 1) * KV_LORA], wuv_ref[h])
    o_ref[...] = acc


def _ouv_call(o_lat, w_uv):
    nb = o_lat.shape[0]
    return pl.pallas_call(
        _ouv_kernel,
        out_shape=jax.ShapeDtypeStruct((nb, ATT_DIM), F32),
    )(o_lat, w_uv)


def _decode_attn_kernel(layer, pt_ref, ql_ref, qr_ref, cn_ref, kn_ref, ckv_hbm, kr_hbm, o_ref, cbuf, kbuf, sem):
    b = pl.program_id(0)
    n_pages = cbuf.shape[1]
    slot = b % 2

    def page_copies(seq, sl, i):
        pid = pt_ref[seq, i]
        return (pltpu.make_async_copy(ckv_hbm.at[layer, pid], cbuf.at[sl, i], sem.at[0, sl]),
                pltpu.make_async_copy(kr_hbm.at[layer, pid],
                                      kbuf.at[sl, :, pl.ds(pl.multiple_of(i * PAGE_SIZE, PAGE_SIZE), PAGE_SIZE)],
                                      sem.at[1, sl]))

    def wait_all(seq, sl):
        def body(i, c):
            for cp in page_copies(seq, sl, i):
                cp.wait()
            return c
        lax.fori_loop(0, n_pages, body, 0)

    @pl.when(b == 0)
    def _():
        def body(i, c):
            for cp in page_copies(0, 0, i):
                cp.start()
            return c
        lax.fori_loop(0, n_pages, body, 0)

    wait_all(b, slot)
    nxt = jnp.minimum(b + 1, pl.num_programs(0) - 1)
    for i in range(n_pages):
        for cp in page_copies(nxt, 1 - slot, i):
            cp.start()

    ql = ql_ref[0]
    qr = qr_ref[0]
    cb = cbuf[slot].reshape(n_pages * PAGE_SIZE, KV_LORA).astype(BF16)
    s = _dg(ql.astype(BF16), cb, _NT) + _dg(qr.astype(BF16), kbuf[slot].astype(BF16))
    cn = cn_ref[0]
    kn = kn_ref[0]
    s_n = jnp.sum(ql * cn, axis=-1, keepdims=True) + jnp.sum(qr * kn, axis=-1, keepdims=True)
    m = jnp.maximum(jnp.max(s, axis=-1, keepdims=True), s_n)
    p = jnp.exp2(s - m)
    pn = jnp.exp2(s_n - m)
    pv = _dg(p.astype(BF16), cb) + pn * cn
    o_ref[0] = pv / (jnp.sum(p, axis=-1, keepdims=True) + pn)

    @pl.when(b == pl.num_programs(0) - 1)
    def _():
        wait_all(nxt, 1 - slot)


def _decode_attn_call(q_lat, q_rope, ckv_new, kr_new, cache_ckv, cache_krope_t, page_table, layer):
    nb, n_pages = page_table.shape
    per_b = lambda shape: pl.BlockSpec((1,) + shape, lambda b, pt: (b, 0, 0))
    hbm = pl.BlockSpec(memory_space=pl.ANY)
    grid_spec = pltpu.PrefetchScalarGridSpec(
        num_scalar_prefetch=1,
        grid=(nb,),
        in_specs=[per_b((H_A, KV_LORA)), per_b((H_A, ROPE_DIM)), per_b((1, KV_LORA)), per_b((1, ROPE_DIM)),
                  hbm, hbm],
        out_specs=per_b((H_A, KV_LORA)),
        scratch_shapes=[pltpu.VMEM((2, n_pages, PAGE_SIZE, KV_LORA), F32),
                        pltpu.VMEM((2, ROPE_DIM, n_pages * PAGE_SIZE), F32),
                        pltpu.SemaphoreType.DMA((2, 2))],
    )
    return pl.pallas_call(
        functools.partial(_decode_attn_kernel, layer),
        grid_spec=grid_spec,
        out_shape=jax.ShapeDtypeStruct((nb, H_A, KV_LORA), F32),
        compiler_params=_params(("arbitrary",)),
    )(page_table, q_lat, q_rope, ckv_new, kr_new, cache_ckv, cache_krope_t)


def _rope_tables(pos):
    half = ROPE_DIM // 2
    inv = ROPE_BASE ** (-jnp.arange(half, dtype=F32) / half)
    ang = pos.astype(F32)[:, None] * inv[None, :]
    cos, sin = jnp.cos(ang), jnp.sin(ang)
    z16, z32 = jnp.zeros_like(cos), jnp.zeros((pos.shape[0], ROPE_DIM), F32)
    seg_c = jnp.concatenate([cos, cos], 1)
    seg_m = jnp.concatenate([-sin, z16], 1)
    seg_p = jnp.concatenate([z16, sin], 1)
    lay = lambda seg: jnp.concatenate([seg, z32, seg, z32], 1)
    return lay(seg_c), lay(seg_m), lay(seg_p)


def _layer_weights(l, g_mix_pre, g_mix_post, g_ffn_pre, g_ffn_post, w_in, g_cq, g_ckv, w_uq, w_ukv, mu_shift,
                   w0, w2, a0, a2, g2, k_k, k_a, r_k, lnx_w, lnx_b, w_out, w_up, w_down, g8):
    row = lambda x: x[l].reshape(1, -1)
    wi = w_in[l]
    w_kr = wi[:, Q_LORA + KV_LORA:MLA_COLS]
    z = jnp.zeros_like(w_kr)
    w_in_p = jnp.concatenate([wi[:, :Q_LORA + KV_LORA], w_kr, z, w_kr, z, wi[:, MLA_COLS:]], 1).astype(BF16)
    w_uq_p = jnp.pad(w_uq[l].reshape(Q_LORA, H_A, QK_DIM), ((0, 0), (0, 0), (0, HEAD_PAD - QK_DIM)))
    wkv = w_ukv[l].reshape(KV_LORA, H_A, NOPE_DIM + V_DIM)
    w_uk, w_uv = wkv[..., :NOPE_DIM], wkv[..., NOPE_DIM:]
    w_k_p = jnp.pad(w_uk, ((0, 0), (0, 0), (0, HEAD_PAD - NOPE_DIM)))
    w_uk_t = jnp.pad(jnp.transpose(w_uk, (1, 2, 0)), ((0, 0), (0, HEAD_PAD - NOPE_DIM), (0, 0)))
    eye = jnp.eye(H_A, dtype=F32)
    w_uv_p = (jnp.transpose(w_uv, (1, 0, 2))[:, :, None, :] * eye[:, None, :, None]).reshape(H_A, KV_LORA, ATT_DIM)
    zl = jnp.zeros((DECAY_LORA, RW_DIM), F32)
    return {
        "g_pre": row(g_mix_pre), "g_post": row(g_mix_post), "g_ffn_pre": row(g_ffn_pre),
        "g_ffn_post": row(g_ffn_post), "g_cq": row(g_cq), "g_ckv": row(g_ckv),
        "w_in": w_in_p, "w_uq": w_uq_p.reshape(Q_LORA, QK_PAD).astype(BF16),
        "w_k": w_k_p.reshape(KV_LORA, QK_PAD).astype(BF16), "w_v": w_uv.reshape(KV_LORA, ATT_DIM).astype(BF16),
        "w_uk_t": w_uk_t.astype(BF16), "w_uv_p": w_uv_p.astype(BF16),
        "mu": row(mu_shift), "w0": row(w0), "a0": row(a0), "k_k": row(k_k), "k_a": row(k_a), "r_k": row(r_k),
        "lnx_w": row(lnx_w), "lnx_b": row(lnx_b),
        "w2": jnp.concatenate([w2[l], zl], 0), "a2": jnp.concatenate([zl, a2[l]], 0), "g2": g2[l], "g8": g8,
        "w_out_att": w_out[l, :ATT_DIM].astype(BF16), "w_out_rw": w_out[l, ATT_DIM:].astype(BF16),
        "w_up": w_up[l].astype(BF16), "w_down": w_down[l].astype(BF16),
    }


def kernel(x_prompt, x_sample, cache_ckv, cache_krope, state_wkv, state_shift, page_table, meta_tokens,
           g_mix_pre, g_mix_post, g_ffn_pre, g_ffn_post, w_in, g_cq, g_ckv, w_uq, w_ukv, mu_shift, w0, w2,
           a0, a2, g2, k_k, k_a, r_k, lnx_w, lnx_b, w_out, w_up, w_down):
    bp, seq, _ = x_prompt.shape
    bd, dec_seq, _ = x_sample.shape
    depth = w_in.shape[0]
    assert dec_seq == 1, "the decode kernels handle one new token per sequence"
    seq_real = seq + N_META
    seq_pad = -(-seq_real // ATT_BLK) * ATT_BLK
    assert seq_pad % RW_STEP == 0 and seq_pad % PROMPT_TILE == 0 and (bp * seq_pad) % FINISH_TILE == 0
    past_len = page_table.shape[1] * cache_ckv.shape[2]

    meta = jnp.broadcast_to(meta_tokens[None].astype(x_prompt.dtype), (bp, N_META, D_MODEL))
    h_p = jnp.pad(x_prompt, ((0, 0), (N_META, seq_pad - seq_real), (0, 0)))
    h_p = lax.dynamic_update_slice(h_p, meta, (0, 0, 0)).reshape(bp * seq_pad, D_MODEL)
    h_s = x_sample.reshape(bd, D_MODEL)
    tab_p = _rope_tables(jnp.arange(seq_pad))
    tab_s = _rope_tables(jnp.full((bd,), past_len, jnp.int32))
    hi = lax.broadcasted_iota(jnp.int32, (RW_DIM, RW_DIM), 0) // N_R
    hj = lax.broadcasted_iota(jnp.int32, (RW_DIM, RW_DIM), 1) // N_R
    g8 = (hi == hj).astype(BF16)
    cache_krope_t = jnp.swapaxes(cache_krope, 2, 3)
    state_t = jnp.transpose(state_wkv, (0, 2, 3, 4, 1))

    outs = [[] for _ in range(8)]
    p_lat = [jnp.zeros((depth, bp, seq_pad, n), F32) for n in (KV_LORA, ROPE_DIM)]
    s_lat = [jnp.zeros((depth, 1, bd, n), F32) for n in (KV_LORA, ROPE_DIM)]
    s_wkv = jnp.zeros(state_t.shape, F32)
    for l in range(depth):
        lw = _layer_weights(l, g_mix_pre, g_mix_post, g_ffn_pre, g_ffn_post, w_in, g_cq, g_ckv, w_uq, w_ukv,
                            mu_shift, w0, w2, a0, a2, g2, k_k, k_a, r_k, lnx_w, lnx_b, w_out, w_up, w_down, g8)
        q, k, v, *p_lat, rw = _proj_call(h_p, lw, tab_p, PROMPT_TILE, l, p_lat)
        o_att = _attn_call(q, k, v, bp, seq_pad)
        o_rw, st = _rwkv_prompt_call(rw, lw, bp, seq_pad, seq_real)
        h_p = _finish_call(h_p, o_att, o_rw, lw, FINISH_TILE)
        outs[2].append(jnp.swapaxes(st, -1, -2))
        outs[3].append(rw.reshape(bp, seq_pad, RW_COLS)[:, seq_real - 1])
        q, _, _, *s_lat, rw = _proj_call(h_s, lw, tab_s, bd, l, s_lat)
        q_lat = _qlat_call(q, lw["w_uk_t"]).reshape(bd, H_A, KV_LORA)
        q_rope = q.reshape(bd, H_A, HEAD_PAD)[:, :, NOPE_DIM:QK_DIM].astype(F32)
        o_lat = _decode_attn_call(q_lat, q_rope, s_lat[0][l].reshape(bd, 1, KV_LORA), s_lat[1][l].reshape(bd, 1, ROPE_DIM),
                                  cache_ckv, cache_krope_t, page_table, l)
        o_att = _ouv_call(o_lat.reshape(bd, H_A * KV_LORA), lw["w_uv_p"])
        o_rw, s_wkv = _rwkv_step_call(rw, state_shift[l], state_t, l, lw, s_wkv)
        h_s = _finish_call(h_s, o_att, o_rw, lw, bd)
        outs[7].append(rw)

    y_prompt = h_p.reshape(bp, seq_pad, D_MODEL)[:, N_META:seq_real]
    y_sample = h_s.reshape(bd, 1, D_MODEL)
    outs = [jnp.stack(o) if o else None for o in outs]
    outs[0], outs[1] = (x[:, :, :seq_real] for x in p_lat)
    outs[4], outs[5] = (x.reshape(depth, bd, 1, -1) for x in s_lat)
    outs[6] = jnp.transpose(s_wkv, (0, 4, 1, 2, 3))
    return (y_prompt, y_sample) + tuple(outs)
```
